```python
import jax, jax.numpy as jnp
from jax import lax
import numpy as np

D_MODEL = 4096
BATCH = 2
SEQ = 4096
DEPTH = 4

CHUNK = 64
Q_BLOCK = 128
N_MIXERS = 2
FOX_HEADS = 32
FOX_HEAD_DIM = D_MODEL // FOX_HEADS
SWA_HEAD_DIM = 64
SWA_HEADS = D_MODEL // SWA_HEAD_DIM
SWA_KV_HEADS = SWA_HEADS // 8
SWA_GROUP = SWA_HEADS // SWA_KV_HEADS
WINDOW = 128
WINDOW_CHUNKS = WINDOW // CHUNK
BAND_PAD = WINDOW_CHUNKS * CHUNK
BAND = Q_BLOCK + BAND_PAD
D_FF = ((8 * D_MODEL // 3 + 255) // 256) * 256
CONV_WIDTH = 3
N_FOX_LAYERS = (DEPTH + 1) // 2
N_SWA_LAYERS = DEPTH // 2
EPS = 1e-6
NEG = -1e30

kernel_name = "hybrid_fox_swa_sink_convffn"


def rms_norm(x, g):
    xf = x.astype(jnp.float32)
    y = xf * lax.rsqrt(jnp.mean(xf * xf, axis=-1, keepdims=True) + EPS)
    return (y * g.astype(jnp.float32)).astype(x.dtype)


def alibi_slopes(n_heads):
    return jnp.exp2(-8.0 * jnp.arange(1, n_heads + 1, dtype=jnp.float32) / n_heads)


def fox_attention(h, w_in, b_f, w_o):
    B, S, _ = h.shape
    proj = h @ w_in
    q = proj[..., :D_MODEL].reshape(B, S, FOX_HEADS, FOX_HEAD_DIM)
    k = proj[..., D_MODEL:2 * D_MODEL].reshape(B, S, FOX_HEADS, FOX_HEAD_DIM)
    v = proj[..., 2 * D_MODEL:3 * D_MODEL].reshape(B, S, FOX_HEADS, FOX_HEAD_DIM)
    f_logit = proj[..., 3 * D_MODEL:].astype(jnp.float32) + b_f.astype(jnp.float32)
    log_f = jax.nn.log_sigmoid(f_logit)
    cum = jnp.cumsum(log_f, axis=1)
    cum = jnp.transpose(cum, (0, 2, 1))
    scale = FOX_HEAD_DIM ** -0.5
    outs = []
    for blk in range(S // Q_BLOCK):
        lo, hi = blk * Q_BLOCK, (blk + 1) * Q_BLOCK
        qb = q[:, lo:hi]
        kb = k[:, :hi]
        vb = v[:, :hi]
        s = jnp.einsum('bqhd,bshd->bhqs', qb, kb).astype(jnp.float32) * scale
        s = s + cum[:, :, lo:hi, None] - cum[:, :, None, :hi]
        mask = jnp.arange(lo, hi)[:, None] >= jnp.arange(hi)[None, :]
        s = jnp.where(mask, s, NEG)
        p = jax.nn.softmax(s, axis=-1).astype(v.dtype)
        outs.append(jnp.einsum('bhqs,bshd->bqhd', p, vb))
    o = jnp.concatenate(outs, axis=1).reshape(B, S, D_MODEL)
    return o @ w_o


def swa_attention(h, w_in, sinks, w_o):
    B, S, _ = h.shape
    nb = S // Q_BLOCK
    kvw = SWA_KV_HEADS * SWA_HEAD_DIM
    proj = h @ w_in
    q = proj[..., :D_MODEL].reshape(B, nb, Q_BLOCK, SWA_KV_HEADS, SWA_GROUP, SWA_HEAD_DIM)
    k = proj[..., D_MODEL:D_MODEL + kvw].reshape(B, S, SWA_KV_HEADS, SWA_HEAD_DIM)
    v = proj[..., D_MODEL + kvw:].reshape(B, S, SWA_KV_HEADS, SWA_HEAD_DIM)
    kp = jnp.pad(k, ((0, 0), (BAND_PAD, 0), (0, 0), (0, 0)))
    vp = jnp.pad(v, ((0, 0), (BAND_PAD, 0), (0, 0), (0, 0)))
    idx = (jnp.arange(nb) * Q_BLOCK)[:, None] + jnp.arange(BAND)[None, :]
    kb = kp[:, idx]
    vb = vp[:, idx]
    qpos = (jnp.arange(nb) * Q_BLOCK)[:, None] + jnp.arange(Q_BLOCK)[None, :]
    kpos = idx - BAND_PAD
    qc = qpos // CHUNK
    kc = kpos // CHUNK
    valid = ((kpos[:, None, :] >= 0)
             & (kc[:, None, :] <= qc[:, :, None])
             & (kc[:, None, :] >= qc[:, :, None] - WINDOW_CHUNKS))
    dist = jnp.abs(qpos[:, :, None] - kpos[:, None, :]).astype(jnp.float32)
    slopes = alibi_slopes(SWA_HEADS).reshape(SWA_KV_HEADS, SWA_GROUP)
    scale = SWA_HEAD_DIM ** -0.5
    s = jnp.einsum('bnqkgd,bnskd->bnkgqs', q, kb).astype(jnp.float32) * scale
    s = s - slopes[:, :, None, None] * dist[:, None, None]
    s = jnp.where(valid[:, None, None], s, NEG)
    sink = jnp.broadcast_to(
        sinks.astype(jnp.float32).reshape(SWA_KV_HEADS, SWA_GROUP)[None, None, :, :, None, None],
        s.shape[:-1] + (1,))
    p = jax.nn.softmax(jnp.concatenate([s, sink], axis=-1), axis=-1)[..., :BAND]
    p = p.astype(v.dtype)
    o = jnp.einsum('bnkgqs,bnskd->bnqkgd', p, vb).reshape(B, S, D_MODEL)
    return o @ w_o


def conv_ffn(h, w_up, conv_w, conv_b, w_down):
    S = h.shape[1]
    u = h @ w_up
    up = jnp.pad(u, ((0, 0), (CONV_WIDTH - 1, 0), (0, 0)))
    c = conv_b
    for tap in range(CONV_WIDTH):
        c = c + conv_w[tap] * up[:, tap:tap + S]
    gate, val = jnp.split(c, 2, axis=-1)
    return (jax.nn.silu(gate) * val) @ w_down


def setup_inputs(seed: int = 0) -> dict:
    key = jax.random.key(seed)
    ks = jax.random.split(key, 14)
    f32 = jnp.float32
    fox_in_w = 3 * D_MODEL + FOX_HEADS
    swa_in_w = D_MODEL + 2 * SWA_KV_HEADS * SWA_HEAD_DIM
    return {
        "x": jax.random.normal(ks[0], (BATCH, SEQ, D_MODEL), f32),
        "attn_norm_g": 1.0 + 0.02 * jax.random.normal(ks[1], (DEPTH, D_MODEL), f32),
        "fox_w_in": jax.random.normal(ks[2], (N_FOX_LAYERS, D_MODEL, fox_in_w), f32) * D_MODEL ** -0.5,
        "fox_b_f": jax.random.uniform(ks[3], (N_FOX_LAYERS, FOX_HEADS), f32, minval=1.0, maxval=6.0),
        "fox_w_o": jax.random.normal(ks[4], (N_FOX_LAYERS, D_MODEL, D_MODEL), f32) * D_MODEL ** -0.5,
        "swa_w_in": jax.random.normal(ks[5], (N_SWA_LAYERS, D_MODEL, swa_in_w), f32) * D_MODEL ** -0.5,
        "swa_sinks": 0.5 * jax.random.normal(ks[6], (N_SWA_LAYERS, SWA_HEADS), f32),
        "swa_w_o": jax.random.normal(ks[7], (N_SWA_LAYERS, D_MODEL, D_MODEL), f32) * D_MODEL ** -0.5,
        "ffn_norm_g": 1.0 + 0.02 * jax.random.normal(ks[8], (DEPTH, D_MODEL), f32),
        "ffn_w_up": jax.random.normal(ks[9], (DEPTH, D_MODEL, 2 * D_FF), f32) * D_MODEL ** -0.5,
        "ffn_conv_w": jax.random.normal(ks[10], (DEPTH, CONV_WIDTH, 2 * D_FF), f32) * CONV_WIDTH ** -0.5,
        "ffn_conv_b": 0.02 * jax.random.normal(ks[11], (DEPTH, 2 * D_FF), f32),
        "ffn_w_down": jax.random.normal(ks[12], (DEPTH, D_FF, D_MODEL), f32) * D_FF ** -0.5,
        "final_norm_g": 1.0 + 0.02 * jax.random.normal(ks[13], (D_MODEL,), f32),
    }


def reference(x, attn_norm_g, fox_w_in, fox_b_f, fox_w_o, swa_w_in, swa_sinks, swa_w_o,
              ffn_norm_g, ffn_w_up, ffn_conv_w, ffn_conv_b, ffn_w_down, final_norm_g):
    h = x
    for layer in range(DEPTH):
        inst = layer // N_MIXERS
        hn = rms_norm(h, attn_norm_g[layer])
        if layer % N_MIXERS == 0:
            mix = fox_attention(hn, fox_w_in[inst], fox_b_f[inst], fox_w_o[inst])
        else:
            mix = swa_attention(hn, swa_w_in[inst], swa_sinks[inst], swa_w_o[inst])
        h = h + mix
        h = h + conv_ffn(rms_norm(h, ffn_norm_g[layer]), ffn_w_up[layer], ffn_conv_w[layer],
                         ffn_conv_b[layer], ffn_w_down[layer])
    return rms_norm(h, final_norm_g)
```

```python
import functools

import jax
import jax.numpy as jnp
from jax import lax
from jax.experimental import pallas as pl
from jax.experimental.pallas import tpu as pltpu

LANE = 128
SUBLANE = 8
VMEM_LIMIT_BYTES = 56 * 1024 * 1024

FOX_HEAD_DIM = 128
SWA_HEAD_DIM = 64
SWA_GROUP = 8
Q_BLOCK = 128
CHUNK = 64
WINDOW_CHUNKS = 2
CONV_WIDTH = 3
EPS = 1e-6
NEG = -1e30

BF16 = jnp.bfloat16
F32 = jnp.float32

_NT = (((1,), (1,)), ((), ()))


def _params(*semantics):
    return pltpu.CompilerParams(
        dimension_semantics=semantics, vmem_limit_bytes=VMEM_LIMIT_BYTES)


def _rmsnorm_kernel(x_ref, g_ref, o_ref):
    x = x_ref[...]
    ms = jnp.mean(x * x, axis=-1, keepdims=True)
    o_ref[...] = (x * lax.rsqrt(ms + EPS) * g_ref[...]).astype(o_ref.dtype)


def _rmsnorm(x, g, out_dtype, *, rows=256):
    m, d = x.shape
    assert m % rows == 0 and d % LANE == 0, (x.shape, rows)
    return pl.pallas_call(
        _rmsnorm_kernel,
        grid=(m // rows,),
        in_specs=[pl.BlockSpec((rows, d), lambda i: (i, 0)),
                  pl.BlockSpec((1, d), lambda i: (0, 0))],
        out_specs=pl.BlockSpec((rows, d), lambda i: (i, 0)),
        out_shape=jax.ShapeDtypeStruct((m, d), out_dtype),
        compiler_params=_params("parallel"),
        name="rmsnorm",
    )(x, g.reshape(1, d))


def _mm_kernel(x_ref, w_ref, o_ref):
    o_ref[...] = jnp.dot(
        x_ref[...], w_ref[...], preferred_element_type=F32).astype(o_ref.dtype)


def _mm_res_kernel(x_ref, w_ref, r_ref, o_ref):
    acc = jnp.dot(x_ref[...], w_ref[...], preferred_element_type=F32)
    o_ref[...] = (r_ref[...] + acc).astype(o_ref.dtype)


def _mm(x, w, out_dtype, *, tm, tn, residual=None, name="mm"):
    m, k = x.shape
    n = w.shape[1]
    assert w.shape[0] == k and m % tm == 0 and n % tn == 0, (x.shape, w.shape, tm, tn)
    in_specs = [pl.BlockSpec((tm, k), lambda i, j: (i, 0)),
                pl.BlockSpec((k, tn), lambda i, j: (0, j))]
    args = [x, w]
    body = _mm_kernel
    if residual is not None:
        in_specs.append(pl.BlockSpec((tm, tn), lambda i, j: (i, j)))
        args.append(residual)
        body = _mm_res_kernel
    return pl.pallas_call(
        body,
        grid=(m // tm, n // tn),
        in_specs=in_specs,
        out_specs=pl.BlockSpec((tm, tn), lambda i, j: (i, j)),
        out_shape=jax.ShapeDtypeStruct((m, n), out_dtype),
        compiler_params=_params("parallel", "arbitrary"),
        name=name,
    )(*args)


def _gate_kernel(x_ref, wt_ref, b_ref, o_ref, carry_ref, *, ts):
    c = pl.program_id(1)

    @pl.when(c == 0)
    def _():
        carry_ref[...] = jnp.zeros_like(carry_ref)

    f = lax.dot_general(wt_ref[...], x_ref[...], _NT,
                        preferred_element_type=F32) + b_ref[...]
    s = jnp.minimum(f, 0.0) - jnp.log1p(jnp.exp(-jnp.abs(f)))
    lane = lax.broadcasted_iota(jnp.int32, s.shape, 1)
    shift = 1
    while shift < ts:
        s = s + jnp.where(lane >= shift, pltpu.roll(s, shift, axis=1), 0.0)
        shift *= 2
    s = s + carry_ref[...]
    o_ref[0] = s
    carry_ref[...] = s[:, ts - 1:ts]


def _fox_gate(hn, wt, b, *, batch, seq, ts=1024):
    heads, d = wt.shape
    assert seq % ts == 0 and heads % SUBLANE == 0, (seq, ts, heads)
    nchunk = seq // ts
    return pl.pallas_call(
        functools.partial(_gate_kernel, ts=ts),
        grid=(batch, nchunk),
        in_specs=[pl.BlockSpec((ts, d), lambda bi, c: (bi * nchunk + c, 0)),
                  pl.BlockSpec((heads, d), lambda bi, c: (0, 0)),
                  pl.BlockSpec((heads, 1), lambda bi, c: (0, 0))],
        out_specs=pl.BlockSpec((1, heads, ts), lambda bi, c: (bi, 0, c)),
        out_shape=jax.ShapeDtypeStruct((batch, heads, seq), F32),
        scratch_shapes=[pltpu.VMEM((heads, 1), F32)],
        compiler_params=_params("parallel", "arbitrary"),
        name="fox_gate",
    )(hn, wt, b.reshape(heads, 1))


def _row_to_col(row):
    n = row.shape[1]
    r = lax.broadcasted_iota(jnp.int32, (LANE, LANE), 0)
    c = lax.broadcasted_iota(jnp.int32, (LANE, LANE), 1)
    cols = []
    for t in range(n // LANE):
        piece = row[:, t * LANE:(t + 1) * LANE]
        cols.append(jnp.sum(jnp.where(r == c, piece, 0.0), axis=1, keepdims=True))
    return jnp.concatenate(cols, axis=0)


def _fox_kernel(q_ref, k_ref, v_ref, cq_ref, ck_ref, o_ref, m_ref, l_ref, acc_ref,
                *, tq, scale):
    i = pl.program_id(2)
    q = q_ref[...]
    ct = _row_to_col(cq_ref[0])
    m_ref[...] = jnp.full_like(m_ref, -jnp.inf)
    l_ref[...] = jnp.zeros_like(l_ref)
    acc_ref[...] = jnp.zeros_like(acc_ref)

    def step(j, masked):
        start = pl.multiple_of(j * tq, tq)
        kj = k_ref[pl.ds(start, tq), :]
        vj = v_ref[pl.ds(start, tq), :]
        s = lax.dot_general(q, kj, _NT, preferred_element_type=F32) * scale
        s = s + ct - ck_ref[0, pl.ds(j, 1), :]
        if masked:
            r = lax.broadcasted_iota(jnp.int32, s.shape, 0)
            c = lax.broadcasted_iota(jnp.int32, s.shape, 1)
            s = jnp.where(r >= c, s, NEG)
        m_prev = m_ref[...]
        m_new = jnp.maximum(m_prev, jnp.max(s, axis=1, keepdims=True))
        alpha = jnp.exp(m_prev - m_new)
        p = jnp.exp(s - m_new)
        l_ref[...] = alpha * l_ref[...] + jnp.sum(p, axis=1, keepdims=True)
        acc_ref[...] = alpha * acc_ref[...] + jnp.dot(
            p.astype(BF16), vj, preferred_element_type=F32)
        m_ref[...] = m_new

    def body(j, carry):
        step(j, masked=False)
        return carry

    lax.fori_loop(0, i, body, 0)
    step(i, masked=True)
    o_ref[...] = (acc_ref[...] / l_ref[...]).astype(o_ref.dtype)


def _fox_attention(qkv, cum, *, batch, seq, heads, tq=512):
    d = FOX_HEAD_DIM
    assert seq % tq == 0 and qkv.shape == (batch * seq, 3 * heads * d), (qkv.shape, seq, tq)
    nq = seq // tq
    cum_q = cum.reshape(batch * heads, 1, seq)
    cum_k = cum.reshape(batch * heads, nq, tq)
    return pl.pallas_call(
        functools.partial(_fox_kernel, tq=tq, scale=d ** -0.5),
        grid=(batch, heads, nq),
        in_specs=[
            pl.BlockSpec((tq, d), lambda b, h, i: (b * nq + i, h)),
            pl.BlockSpec((seq, d), lambda b, h, i: (b, heads + h)),
            pl.BlockSpec((seq, d), lambda b, h, i: (b, 2 * heads + h)),
            pl.BlockSpec((1, 1, tq), lambda b, h, i: (b * heads + h, 0, i)),
            pl.BlockSpec((1, nq, tq), lambda b, h, i: (b * heads + h, 0, 0)),
        ],
        out_specs=pl.BlockSpec((tq, d), lambda b, h, i: (b * nq + i, h)),
        out_shape=jax.ShapeDtypeStruct((batch * seq, heads * d), BF16),
        scratch_shapes=[pltpu.VMEM((tq, 1), F32), pltpu.VMEM((tq, 1), F32),
                        pltpu.VMEM((tq, d), F32)],
        compiler_params=_params("parallel", "parallel", "arbitrary"),
        name="fox_attention",
    )(qkv, qkv, qkv, cum_q, cum_k)


def _swa_kernel(slope_ref, sink_ref, q_ref, kp_ref, kc_ref, vp_ref, vc_ref, o_ref,
                *, scale):
    n = pl.program_id(1)
    pair = pl.program_id(2)
    band = 2 * Q_BLOCK
    half = SWA_HEAD_DIM

    kband = jnp.concatenate([kp_ref[...], kc_ref[...]], axis=0).astype(F32)
    vband = jnp.concatenate([vp_ref[...], vc_ref[...]], axis=0).astype(F32)
    kroll = pltpu.roll(kband, half, axis=1)
    vroll = pltpu.roll(vband, half, axis=1)
    lo = lax.broadcasted_iota(jnp.int32, kband.shape, 1) < half

    r = lax.broadcasted_iota(jnp.int32, (Q_BLOCK, band), 0)
    c = lax.broadcasted_iota(jnp.int32, (Q_BLOCK, band), 1)
    dist = jnp.abs(r - (c - Q_BLOCK)).astype(F32)
    qc = r // CHUNK
    kc = c // CHUNK - WINDOW_CHUNKS
    valid = (kc <= qc) & (kc >= qc - WINDOW_CHUNKS) & ((c >= Q_BLOCK) | (n > 0))

    for cc in range(2):
        k_src, k_alt = (kband, kroll) if cc == 0 else (kroll, kband)
        v_src, v_alt = (vband, vroll) if cc == 0 else (vroll, vband)
        k_even = jnp.where(lo, k_src, 0.0).astype(BF16)
        k_odd = jnp.where(lo, 0.0, k_alt).astype(BF16)
        v_even = jnp.where(lo, v_src, 0.0).astype(BF16)
        v_odd = jnp.where(lo, 0.0, v_alt).astype(BF16)

        npair = SWA_GROUP // 2
        qs = jnp.concatenate(
            [q_ref[:, (cc * npair + t) * LANE:(cc * npair + t + 1) * LANE]
             for t in range(npair)], axis=0)
        scores = (lax.dot_general(qs, k_even, _NT, preferred_element_type=F32),
                  lax.dot_general(qs, k_odd, _NT, preferred_element_type=F32))
        probs = ([], [])
        for t in range(npair):
            for par in range(2):
                head = (2 * pair + cc) * SWA_GROUP + 2 * t + par
                slope = slope_ref[head]
                sink = sink_ref[head]
                s = scores[par][t * Q_BLOCK:(t + 1) * Q_BLOCK] * scale - slope * dist
                s = jnp.where(valid, s, NEG)
                m = jnp.maximum(jnp.max(s, axis=1, keepdims=True), sink)
                e = jnp.exp(s - m)
                denom = jnp.sum(e, axis=1, keepdims=True) + jnp.exp(sink - m)
                probs[par].append((e / denom).astype(BF16))
        p_even = jnp.concatenate(probs[0], axis=0)
        p_odd = jnp.concatenate(probs[1], axis=0)
        out = (jnp.dot(p_even, v_even, preferred_element_type=F32)
               + jnp.dot(p_odd, v_odd, preferred_element_type=F32))
        for t in range(npair):
            o_ref[:, (cc * npair + t) * LANE:(cc * npair + t + 1) * LANE] = (
                out[t * Q_BLOCK:(t + 1) * Q_BLOCK].astype(o_ref.dtype))


def _swa_attention(proj, slopes, sinks, *, batch, seq, d_model):
    nb = seq // Q_BLOCK
    kv_heads = d_model // SWA_HEAD_DIM // SWA_GROUP
    npairs = kv_heads // 2
    assert seq % Q_BLOCK == 0 and kv_heads % 2 == 0, (seq, kv_heads)
    assert proj.shape == (batch * seq, d_model + 2 * kv_heads * SWA_HEAD_DIM), proj.shape
    qw = 2 * SWA_GROUP * SWA_HEAD_DIM
    k0 = d_model // LANE
    v0 = k0 + npairs
    prev = lambda b, n, p: b * nb + jnp.maximum(n - 1, 0)
    smem = pl.BlockSpec(memory_space=pltpu.SMEM)
    return pl.pallas_call(
        functools.partial(_swa_kernel, scale=SWA_HEAD_DIM ** -0.5),
        grid=(batch, nb, npairs),
        in_specs=[
            smem, smem,
            pl.BlockSpec((Q_BLOCK, qw), lambda b, n, p: (b * nb + n, p)),
            pl.BlockSpec((Q_BLOCK, LANE), lambda b, n, p: (prev(b, n, p), k0 + p)),
            pl.BlockSpec((Q_BLOCK, LANE), lambda b, n, p: (b * nb + n, k0 + p)),
            pl.BlockSpec((Q_BLOCK, LANE), lambda b, n, p: (prev(b, n, p), v0 + p)),
            pl.BlockSpec((Q_BLOCK, LANE), lambda b, n, p: (b * nb + n, v0 + p)),
        ],
        out_specs=pl.BlockSpec((Q_BLOCK, qw), lambda b, n, p: (b * nb + n, p)),
        out_shape=jax.ShapeDtypeStruct((batch * seq, d_model), BF16),
        compiler_params=_params("parallel", "parallel", "arbitrary"),
        name="swa_attention",
    )(slopes, sinks, proj, proj, proj, proj, proj)


def _up_kernel(x_ref, wg_ref, wv_ref, cwg_ref, cwv_ref, cbg_ref, cbv_ref, o_ref,
               ug_ref, uv_ref, tailg_ref, tailv_ref, *, tm, seq_tiles):
    i = pl.program_id(0)
    j = pl.program_id(1)
    x = x_ref[...]
    seq_start = (i % seq_tiles) == 0

    def conv(w_ref, u_ref, tail_ref, cw_ref, cb_ref):
        u = jnp.dot(x, w_ref[...], preferred_element_type=F32)
        u_ref[0:SUBLANE, :] = jnp.where(seq_start, 0.0, tail_ref[j])
        u_ref[SUBLANE:SUBLANE + tm, :] = u
        tail_ref[j] = u[tm - SUBLANE:tm, :]
        cw = cw_ref[...]
        out = cb_ref[...]
        for tap in range(CONV_WIDTH):
            off = SUBLANE - (CONV_WIDTH - 1) + tap
            out = out + cw[tap:tap + 1, :] * u_ref[off:off + tm, :]
        return out

    gate = conv(wg_ref, ug_ref, tailg_ref, cwg_ref, cbg_ref)
    val = conv(wv_ref, uv_ref, tailv_ref, cwv_ref, cbv_ref)
    o_ref[...] = (gate * (1.0 / (1.0 + jnp.exp(-gate))) * val).astype(o_ref.dtype)


def _conv_ffn_up(hn, w_up, conv_w, conv_b, *, seq, tm=1024, tn=256):
    m, d = hn.shape
    f = w_up.shape[1] // 2
    assert m % tm == 0 and seq % tm == 0 and f % tn == 0, (m, seq, f, tm, tn)
    nj = f // tn
    conv_b = conv_b.reshape(1, 2 * f)
    return pl.pallas_call(
        functools.partial(_up_kernel, tm=tm, seq_tiles=seq // tm),
        grid=(m // tm, nj),
        in_specs=[
            pl.BlockSpec((tm, d), lambda i, j: (i, 0)),
            pl.BlockSpec((d, tn), lambda i, j: (0, j)),
            pl.BlockSpec((d, tn), lambda i, j: (0, nj + j)),
            pl.BlockSpec((CONV_WIDTH, tn), lambda i, j: (0, j)),
            pl.BlockSpec((CONV_WIDTH, tn), lambda i, j: (0, nj + j)),
            pl.BlockSpec((1, tn), lambda i, j: (0, j)),
            pl.BlockSpec((1, tn), lambda i, j: (0, nj + j)),
        ],
        out_specs=pl.BlockSpec((tm, tn), lambda i, j: (i, j)),
        out_shape=jax.ShapeDtypeStruct((m, f), BF16),
        scratch_shapes=[pltpu.VMEM((tm + SUBLANE, tn), F32),
                        pltpu.VMEM((tm + SUBLANE, tn), F32),
                        pltpu.VMEM((nj, SUBLANE, tn), F32),
                        pltpu.VMEM((nj, SUBLANE, tn), F32)],
        compiler_params=_params("arbitrary", "arbitrary"),
        name="conv_ffn_up",
    )(hn, w_up, w_up, conv_w, conv_w, conv_b, conv_b)


def kernel(x, attn_norm_g, fox_w_in, fox_b_f, fox_w_o, swa_w_in, swa_sinks, swa_w_o,
           ffn_norm_g, ffn_w_up, ffn_conv_w, ffn_conv_b, ffn_w_down, final_norm_g):
    batch, seq, d_model = x.shape
    depth = attn_norm_g.shape[0]
    fox_heads = fox_b_f.shape[1]
    swa_heads = swa_sinks.shape[1]
    slopes = jnp.exp2(-8.0 * jnp.arange(1, swa_heads + 1, dtype=F32) / swa_heads)

    h = x.reshape(batch * seq, d_model)
    for layer in range(depth):
        inst = layer // 2
        hn = _rmsnorm(h, attn_norm_g[layer], BF16)
        if layer % 2 == 0:
            w_in = fox_w_in[inst]
            qkv = _mm(hn, w_in[:, :3 * d_model].astype(BF16), BF16,
                      tm=1024, tn=512, name="fox_qkv")
            cum = _fox_gate(hn, w_in[:, 3 * d_model:].T.astype(BF16), fox_b_f[inst],
                            batch=batch, seq=seq)
            mix = _fox_attention(qkv, cum, batch=batch, seq=seq, heads=fox_heads)
            w_o = fox_w_o[inst]
        else:
            proj = _mm(hn, swa_w_in[inst].astype(BF16), BF16,
                       tm=1024, tn=512, name="swa_qkv")
            mix = _swa_attention(proj, slopes, swa_sinks[inst],
                                 batch=batch, seq=seq, d_model=d_model)
            w_o = swa_w_o[inst]
        h = _mm(mix, w_o.astype(BF16), F32, tm=1024, tn=512, residual=h, name="attn_out")
        hn = _rmsnorm(h, ffn_norm_g[layer], BF16)
        act = _conv_ffn_up(hn, ffn_w_up[layer].astype(BF16), ffn_conv_w[layer],
                           ffn_conv_b[layer], seq=seq)
        h = _mm(act, ffn_w_down[layer].astype(BF16), F32, tm=512, tn=256,
                residual=h, name="ffn_down")
    out = _rmsnorm(h, final_norm_g, F32)
    return out.reshape(batch, seq, d_model)
```

```python
import functools

import jax
import jax.numpy as jnp
from jax import lax
from jax.experimental import pallas as pl
from jax.experimental.pallas import tpu as pltpu

LANE = 128
SUBLANE = 8
VMEM_LIMIT_BYTES = 56 * 1024 * 1024

FOX_HEAD_DIM = 128
SWA_HEAD_DIM = 64
SWA_GROUP = 8
Q_BLOCK = 128
CHUNK = 64
WINDOW_CHUNKS = 2
CONV_WIDTH = 3
EPS = 1e-6
NEG = -1e30
LOG2E = 1.4426950408889634

BF16 = jnp.bfloat16
F32 = jnp.float32

_NT = (((1,), (1,)), ((), ()))


def _params(*semantics):
    return pltpu.CompilerParams(
        dimension_semantics=semantics, vmem_limit_bytes=VMEM_LIMIT_BYTES)


def _rmsnorm_kernel(x_ref, g_ref, o_ref):
    x = x_ref[...]
    ms = jnp.mean(x * x, axis=-1, keepdims=True)
    o_ref[...] = (x * lax.rsqrt(ms + EPS) * g_ref[...]).astype(o_ref.dtype)


def _rmsnorm(x, g, out_dtype, *, rows=256):
    m, d = x.shape
    assert m % rows == 0 and d % LANE == 0, (x.shape, rows)
    return pl.pallas_call(
        _rmsnorm_kernel,
        grid=(m // rows,),
        in_specs=[pl.BlockSpec((rows, d), lambda i: (i, 0)),
                  pl.BlockSpec((1, d), lambda i: (0, 0))],
        out_specs=pl.BlockSpec((rows, d), lambda i: (i, 0)),
        out_shape=jax.ShapeDtypeStruct((m, d), out_dtype),
        compiler_params=_params("parallel"),
        name="rmsnorm",
    )(x, g.reshape(1, d))


def _mm_kernel(x_ref, w_ref, o_ref):
    o_ref[...] = jnp.dot(
        x_ref[...], w_ref[...].astype(BF16), preferred_element_type=F32).astype(o_ref.dtype)


def _mm_res_kernel(x_ref, w_ref, r_ref, o_ref):
    acc = jnp.dot(x_ref[...], w_ref[...].astype(BF16), preferred_element_type=F32)
    o_ref[...] = (r_ref[...] + acc).astype(o_ref.dtype)


def _mm(x, w, layer, n, out_dtype, *, tm, tn, residual=None, name="mm"):
    m, k = x.shape
    assert w.shape[1] == k and w.shape[2] >= n, (x.shape, w.shape, n)
    assert m % tm == 0 and n % tn == 0, (m, n, tm, tn)
    in_specs = [pl.BlockSpec((tm, k), lambda i, j: (i, 0)),
                pl.BlockSpec((None, k, tn), lambda i, j: (layer, 0, j))]
    args = [x, w]
    body = _mm_kernel
    if residual is not None:
        in_specs.append(pl.BlockSpec((tm, tn), lambda i, j: (i, j)))
        args.append(residual)
        body = _mm_res_kernel
    return pl.pallas_call(
        body,
        grid=(m // tm, n // tn),
        in_specs=in_specs,
        out_specs=pl.BlockSpec((tm, tn), lambda i, j: (i, j)),
        out_shape=jax.ShapeDtypeStruct((m, n), out_dtype),
        compiler_params=_params("parallel", "arbitrary"),
        name=name,
    )(*args)


def _gate_kernel(x_ref, wt_ref, b_ref, o_ref, carry_ref, *, ts):
    c = pl.program_id(1)

    @pl.when(c == 0)
    def _():
        carry_ref[...] = jnp.zeros_like(carry_ref)

    f = lax.dot_general(wt_ref[...], x_ref[...], _NT,
                        preferred_element_type=F32) + b_ref[...]
    s = jnp.minimum(f, 0.0) - jnp.log1p(jnp.exp(-jnp.abs(f)))
    lane = lax.broadcasted_iota(jnp.int32, s.shape, 1)
    shift = 1
    while shift < ts:
        s = s + jnp.where(lane >= shift, pltpu.roll(s, shift, axis=1), 0.0)
        shift *= 2
    s = s + carry_ref[...]
    o_ref[0] = s
    carry_ref[...] = s[:, ts - 1:ts]


def _fox_gate(hn, wt, b, *, batch, seq, ts=1024):
    heads, d = wt.shape
    assert seq % ts == 0 and heads % SUBLANE == 0, (seq, ts, heads)
    nchunk = seq // ts
    return pl.pallas_call(
        functools.partial(_gate_kernel, ts=ts),
        grid=(batch, nchunk),
        in_specs=[pl.BlockSpec((ts, d), lambda bi, c: (bi * nchunk + c, 0)),
                  pl.BlockSpec((heads, d), lambda bi, c: (0, 0)),
                  pl.BlockSpec((heads, 1), lambda bi, c: (0, 0))],
        out_specs=pl.BlockSpec((1, heads, ts), lambda bi, c: (bi, 0, c)),
        out_shape=jax.ShapeDtypeStruct((batch, heads, seq), F32),
        scratch_shapes=[pltpu.VMEM((heads, 1), F32)],
        compiler_params=_params("parallel", "arbitrary"),
        name="fox_gate",
    )(hn, wt, b.reshape(heads, 1))


def _fox_kernel(q_ref, k_ref, v_ref, cq_ref, ck_ref, o_ref, cs_ref, m_ref, l_ref, acc_ref,
                *, tq, nheads, scale):
    i = pl.program_id(2)
    d = FOX_HEAD_DIM
    nchunk = ck_ref.shape[2]

    @pl.when(i == 0)
    def _():
        r = lax.broadcasted_iota(jnp.int32, (LANE, LANE), 0)
        c = lax.broadcasted_iota(jnp.int32, (LANE, LANE), 1)
        for hh in range(nheads):
            def chunk(t, carry, hh=hh):
                row = ck_ref[0, hh, pl.ds(t, 1), :] * LOG2E
                col = jnp.sum(jnp.where(r == c, row, 0.0), axis=1, keepdims=True)
                cs_ref[hh, pl.ds(pl.multiple_of(t * LANE, LANE), LANE), :] = (
                    jnp.broadcast_to(col, (LANE, LANE)))
                return carry
            lax.fori_loop(0, nchunk, chunk, 0, unroll=8)

    m_ref[...] = jnp.full_like(m_ref, -jnp.inf)
    l_ref[...] = jnp.zeros_like(l_ref)
    acc_ref[...] = jnp.zeros_like(acc_ref)

    def step(j, masked):
        start = pl.multiple_of(j * tq, tq)
        for hh in range(nheads):
            cols = slice(hh * d, (hh + 1) * d)
            kj = k_ref[pl.ds(start, tq), cols]
            vj = v_ref[pl.ds(start, tq), cols]
            s = lax.dot_general(kj, q_ref[:, cols], _NT,
                                preferred_element_type=F32)
            ct = cq_ref[0, hh:hh + 1, :] * LOG2E
            cs = jnp.tile(cs_ref[hh, pl.ds(start, tq), :], (1, tq // LANE))
            x = s * (scale * LOG2E) + ct - cs
            if masked:
                r = lax.broadcasted_iota(jnp.int32, x.shape, 0)
                c = lax.broadcasted_iota(jnp.int32, x.shape, 1)
                x = jnp.where(c >= r, x, NEG)
            m_prev = m_ref[hh]
            m_new = jnp.maximum(m_prev, jnp.max(x, axis=0, keepdims=True))
            alpha = jnp.exp2(m_prev - m_new)
            p = jnp.exp2(x - m_new)
            l_ref[hh] = alpha * l_ref[hh] + jnp.sum(p, axis=0, keepdims=True)
            pv = lax.dot_general(vj, p.astype(BF16), (((0,), (0,)), ((), ())),
                                 preferred_element_type=F32)
            acc_ref[hh] = alpha * acc_ref[hh] + pv
            m_ref[hh] = m_new

    def body(j, carry):
        step(j, masked=False)
        return carry

    lax.fori_loop(0, i, body, 0)
    step(i, masked=True)
    for hh in range(nheads):
        out = acc_ref[hh] / l_ref[hh]
        o_ref[:, hh * d:(hh + 1) * d] = out.T.astype(o_ref.dtype)


def _fox_attention(qkv, cum, *, batch, seq, heads, tq=512, nheads=2):
    d = FOX_HEAD_DIM
    assert seq % tq == 0 and tq % LANE == 0 and heads % nheads == 0, (seq, tq, heads)
    assert qkv.shape == (batch * seq, 3 * heads * d), qkv.shape
    nq = seq // tq
    groups = heads // nheads
    w = nheads * d
    cum_q = cum.reshape(batch * groups, nheads, seq)
    cum_k = cum.reshape(batch * groups, nheads, seq // LANE, LANE)
    return pl.pallas_call(
        functools.partial(_fox_kernel, tq=tq, nheads=nheads, scale=d ** -0.5),
        grid=(batch, groups, nq),
        in_specs=[
            pl.BlockSpec((tq, w), lambda b, g, i: (b * nq + i, g)),
            pl.BlockSpec((seq, w), lambda b, g, i: (b, groups + g)),
            pl.BlockSpec((seq, w), lambda b, g, i: (b, 2 * groups + g)),
            pl.BlockSpec((1, nheads, tq), lambda b, g, i: (b * groups + g, 0, i)),
            pl.BlockSpec((1, nheads, seq // LANE, LANE),
                         lambda b, g, i: (b * groups + g, 0, 0, 0)),
        ],
        out_specs=pl.BlockSpec((tq, w), lambda b, g, i: (b * nq + i, g)),
        out_shape=jax.ShapeDtypeStruct((batch * seq, heads * d), BF16),
        scratch_shapes=[pltpu.VMEM((nheads, seq, LANE), F32),
                        pltpu.VMEM((nheads, 1, tq), F32),
                        pltpu.VMEM((nheads, 1, tq), F32),
                        pltpu.VMEM((nheads, d, tq), F32)],
        compiler_params=_params("parallel", "parallel", "arbitrary"),
        name="fox_attention",
    )(qkv, qkv, qkv, cum_q, cum_k)


def _swa_kernel(slope_ref, sink_ref, q_ref, kp_ref, kc_ref, vp_ref, vc_ref, o_ref,
                *, scale):
    n = pl.program_id(1)
    pair = pl.program_id(2)
    band = 2 * Q_BLOCK
    half = SWA_HEAD_DIM

    kband = jnp.concatenate([kp_ref[...], kc_ref[...]], axis=0).astype(F32)
    vband = jnp.concatenate([vp_ref[...], vc_ref[...]], axis=0).astype(F32)
    kroll = pltpu.roll(kband, half, axis=1)
    vroll = pltpu.roll(vband, half, axis=1)
    lo = lax.broadcasted_iota(jnp.int32, kband.shape, 1) < half

    r = lax.broadcasted_iota(jnp.int32, (Q_BLOCK, band), 0)
    c = lax.broadcasted_iota(jnp.int32, (Q_BLOCK, band), 1)
    dist = jnp.abs(r - (c - Q_BLOCK)).astype(F32)
    qc = r // CHUNK
    kc = c // CHUNK - WINDOW_CHUNKS
    valid = (kc <= qc) & (kc >= qc - WINDOW_CHUNKS) & ((c >= Q_BLOCK) | (n > 0))

    for cc in range(2):
        k_src, k_alt = (kband, kroll) if cc == 0 else (kroll, kband)
        v_src, v_alt = (vband, vroll) if cc == 0 else (vroll, vband)
        k_even = jnp.where(lo, k_src, 0.0).astype(BF16)
        k_odd = jnp.where(lo, 0.0, k_alt).astype(BF16)
        v_even = jnp.where(lo, v_src, 0.0).astype(BF16)
        v_odd = jnp.where(lo, 0.0, v_alt).astype(BF16)

        npair = SWA_GROUP // 2
        qs = jnp.concatenate(
            [q_ref[:, (cc * npair + t) * LANE:(cc * npair + t + 1) * LANE]
             for t in range(npair)], axis=0)
        scores = (lax.dot_general(qs, k_even, _NT, preferred_element_type=F32),
                  lax.dot_general(qs, k_odd, _NT, preferred_element_type=F32))
        probs = ([], [])
        for t in range(npair):
            for par in range(2):
                head = (2 * pair + cc) * SWA_GROUP + 2 * t + par
                slope = slope_ref[head]
                sink = sink_ref[head]
                s = scores[par][t * Q_BLOCK:(t + 1) * Q_BLOCK] * scale - slope * dist
                s = jnp.where(valid, s, NEG)
                m = jnp.maximum(jnp.max(s, axis=1, keepdims=True), sink)
                e = jnp.exp(s - m)
                denom = jnp.sum(e, axis=1, keepdims=True) + jnp.exp(sink - m)
                probs[par].append((e / denom).astype(BF16))
        p_even = jnp.concatenate(probs[0], axis=0)
        p_odd = jnp.concatenate(probs[1], axis=0)
        out = (jnp.dot(p_even, v_even, preferred_element_type=F32)
               + jnp.dot(p_odd, v_odd, preferred_element_type=F32))
        for t in range(npair):
            o_ref[:, (cc * npair + t) * LANE:(cc * npair + t + 1) * LANE] = (
                out[t * Q_BLOCK:(t + 1) * Q_BLOCK].astype(o_ref.dtype))


def _swa_attention(proj, slopes, sinks, *, batch, seq, d_model):
    nb = seq // Q_BLOCK
    kv_heads = d_model // SWA_HEAD_DIM // SWA_GROUP
    npairs = kv_heads // 2
    assert seq % Q_BLOCK == 0 and kv_heads % 2 == 0, (seq, kv_heads)
    assert proj.shape == (batch * seq, d_model + 2 * kv_heads * SWA_HEAD_DIM), proj.shape
    qw = 2 * SWA_GROUP * SWA_HEAD_DIM
    k0 = d_model // LANE
    v0 = k0 + npairs
    prev = lambda b, n, p: b * nb + jnp.maximum(n - 1, 0)
    smem = pl.BlockSpec(memory_space=pltpu.SMEM)
    return pl.pallas_call(
        functools.partial(_swa_kernel, scale=SWA_HEAD_DIM ** -0.5),
        grid=(batch, nb, npairs),
        in_specs=[
            smem, smem,
            pl.BlockSpec((Q_BLOCK, qw), lambda b, n, p: (b * nb + n, p)),
            pl.BlockSpec((Q_BLOCK, LANE), lambda b, n, p: (prev(b, n, p), k0 + p)),
            pl.BlockSpec((Q_BLOCK, LANE), lambda b, n, p: (b * nb + n, k0 + p)),
            pl.BlockSpec((Q_BLOCK, LANE), lambda b, n, p: (prev(b, n, p), v0 + p)),
            pl.BlockSpec((Q_BLOCK, LANE), lambda b, n, p: (b * nb + n, v0 + p)),
        ],
        out_specs=pl.BlockSpec((Q_BLOCK, qw), lambda b, n, p: (b * nb + n, p)),
        out_shape=jax.ShapeDtypeStruct((batch * seq, d_model), BF16),
        compiler_params=_params("parallel", "parallel", "arbitrary"),
        name="swa_attention",
    )(slopes, sinks, proj, proj, proj, proj, proj)


def _up_kernel(x_ref, wg_ref, wv_ref, cwg_ref, cwv_ref, cbg_ref, cbv_ref, o_ref,
               ug_ref, uv_ref, tailg_ref, tailv_ref, *, tm, seq_tiles, row_chunks):
    i = pl.program_id(0)
    j = pl.program_id(1)
    seq_start = (i % seq_tiles) == 0
    tc = tm // row_chunks
    sides = ((wg_ref[...].astype(BF16), ug_ref, tailg_ref, cwg_ref[...], cbg_ref[...]),
             (wv_ref[...].astype(BF16), uv_ref, tailv_ref, cwv_ref[...], cbv_ref[...]))

    for _, u_ref, tail_ref, _, _ in sides:
        u_ref[0:SUBLANE, :] = jnp.where(seq_start, 0.0, tail_ref[j])
    for c in range(row_chunks):
        r0 = c * tc
        x = x_ref[r0:r0 + tc, :]
        conv = []
        for w, u_ref, _, cw, cb in sides:
            u_ref[SUBLANE + r0:SUBLANE + r0 + tc, :] = jnp.dot(
                x, w, preferred_element_type=F32)
            out = cb
            for tap in range(CONV_WIDTH):
                off = SUBLANE - (CONV_WIDTH - 1) + tap + r0
                out = out + cw[tap:tap + 1, :] * u_ref[off:off + tc, :]
            conv.append(out)
        gate, val = conv
        o_ref[r0:r0 + tc, :] = (
            gate * (1.0 / (1.0 + jnp.exp(-gate))) * val).astype(o_ref.dtype)
    for _, u_ref, tail_ref, _, _ in sides:
        tail_ref[j] = u_ref[tm:tm + SUBLANE, :]


def _conv_ffn_up(hn, w_up, conv_w, conv_b, layer, *, seq, tm=1024, tn=256, row_chunks=1):
    m, d = hn.shape
    f = w_up.shape[2] // 2
    assert m % tm == 0 and seq % tm == 0 and f % tn == 0, (m, seq, f, tm, tn)
    nj = f // tn
    conv_b = conv_b.reshape(conv_b.shape[0], 1, 2 * f)
    gate = lambda i, j: (layer, 0, j)
    val = lambda i, j: (layer, 0, nj + j)
    return pl.pallas_call(
        functools.partial(_up_kernel, tm=tm, seq_tiles=seq // tm, row_chunks=row_chunks),
        grid=(m // tm, nj),
        in_specs=[
            pl.BlockSpec((tm, d), lambda i, j: (i, 0)),
            pl.BlockSpec((None, d, tn), gate),
            pl.BlockSpec((None, d, tn), val),
            pl.BlockSpec((None, CONV_WIDTH, tn), gate),
            pl.BlockSpec((None, CONV_WIDTH, tn), val),
            pl.BlockSpec((None, 1, tn), gate),
            pl.BlockSpec((None, 1, tn), val),
        ],
        out_specs=pl.BlockSpec((tm, tn), lambda i, j: (i, j)),
        out_shape=jax.ShapeDtypeStruct((m, f), BF16),
        scratch_shapes=[pltpu.VMEM((tm + SUBLANE, tn), F32),
                        pltpu.VMEM((tm + SUBLANE, tn), F32),
                        pltpu.VMEM((nj, SUBLANE, tn), F32),
                        pltpu.VMEM((nj, SUBLANE, tn), F32)],
        compiler_params=_params("arbitrary", "arbitrary"),
        name="conv_ffn_up",
    )(hn, w_up, w_up, conv_w, conv_w, conv_b, conv_b)


def kernel(x, attn_norm_g, fox_w_in, fox_b_f, fox_w_o, swa_w_in, swa_sinks, swa_w_o,
           ffn_norm_g, ffn_w_up, ffn_conv_w, ffn_conv_b, ffn_w_down, final_norm_g):
    batch, seq, d_model = x.shape
    depth = attn_norm_g.shape[0]
    fox_heads = fox_b_f.shape[1]
    swa_heads = swa_sinks.shape[1]
    slopes = jnp.exp2(-8.0 * jnp.arange(1, swa_heads + 1, dtype=F32) / swa_heads)

    h = x.reshape(batch * seq, d_model)
    for layer in range(depth):
        inst = layer // 2
        hn = _rmsnorm(h, attn_norm_g[layer], BF16)
        if layer % 2 == 0:
            qkv = _mm(hn, fox_w_in, inst, 3 * d_model, BF16, tm=1024, tn=512, name="fox_qkv")
            w_gate = fox_w_in[inst, :, 3 * d_model:].T.astype(BF16)
            cum = _fox_gate(hn, w_gate, fox_b_f[inst], batch=batch, seq=seq)
            mix = _fox_attention(qkv, cum, batch=batch, seq=seq, heads=fox_heads)
            w_o = fox_w_o
        else:
            proj = _mm(hn, swa_w_in, inst, swa_w_in.shape[2], BF16,
                       tm=1024, tn=512, name="swa_qkv")
            mix = _swa_attention(proj, slopes, swa_sinks[inst],
                                 batch=batch, seq=seq, d_model=d_model)
            w_o = swa_w_o
        h = _mm(mix, w_o, inst, d_model, F32, tm=1024, tn=512, residual=h, name="attn_out")
        hn = _rmsnorm(h, ffn_norm_g[layer], BF16)
        act = _conv_ffn_up(hn, ffn_w_up, ffn_conv_w, ffn_conv_b, layer, seq=seq)
        h = _mm(act, ffn_w_down.astype(BF16), layer, d_model, F32, tm=512, tn=256,
                residual=h, name="ffn_down")
    out = _rmsnorm(h, final_norm_g, F32)
    return out.reshape(batch, seq, d_model)
```

```python
import functools

import jax
import jax.numpy as jnp
from jax import lax
from jax.experimental import pallas as pl
from jax.experimental.pallas import tpu as pltpu

LANE = 128
SUBLANE = 8
VMEM_LIMIT_BYTES = 56 * 1024 * 1024

FOX_HEAD_DIM = 128
SWA_HEAD_DIM = 64
SWA_GROUP = 8
Q_BLOCK = 128
CHUNK = 64
WINDOW_CHUNKS = 2
CONV_WIDTH = 3
EPS = 1e-6
NEG = -1e30
LOG2E = 1.4426950408889634

BF16 = jnp.bfloat16
F32 = jnp.float32

_NT = (((1,), (1,)), ((), ()))


def _params(*semantics):
    return pltpu.CompilerParams(
        dimension_semantics=semantics, vmem_limit_bytes=VMEM_LIMIT_BYTES)


def _rmsnorm_kernel(x_ref, g_ref, o_ref):
    x = x_ref[...]
    ms = jnp.mean(x * x, axis=-1, keepdims=True)
    o_ref[...] = (x * lax.rsqrt(ms + EPS) * g_ref[...]).astype(o_ref.dtype)


def _rmsnorm(x, g, out_dtype, *, rows=256):
    m, d = x.shape
    assert m % rows == 0 and d % LANE == 0, (x.shape, rows)
    return pl.pallas_call(
        _rmsnorm_kernel,
        grid=(m // rows,),
        in_specs=[pl.BlockSpec((rows, d), lambda i: (i, 0)),
                  pl.BlockSpec((1, d), lambda i: (0, 0))],
        out_specs=pl.BlockSpec((rows, d), lambda i: (i, 0)),
        out_shape=jax.ShapeDtypeStruct((m, d), out_dtype),
        compiler_params=_params("parallel"),
        name="rmsnorm",
    )(x, g.reshape(1, d))


def _mm_kernel(x_ref, w_ref, o_ref):
    o_ref[...] = jnp.dot(
        x_ref[...], w_ref[...].astype(BF16), preferred_element_type=F32).astype(o_ref.dtype)


def _mm_res_kernel(x_ref, w_ref, r_ref, o_ref):
    acc = jnp.dot(x_ref[...], w_ref[...].astype(BF16), preferred_element_type=F32)
    o_ref[...] = (r_ref[...] + acc).astype(o_ref.dtype)


def _mm(x, w, layer, n, out_dtype, *, tm, tn, residual=None, single_buffer_x=False, name="mm"):
    m, k = x.shape
    assert w.shape[1] == k and w.shape[2] >= n, (x.shape, w.shape, n)
    assert m % tm == 0 and n % tn == 0, (m, n, tm, tn)
    x_mode = dict(pipeline_mode=pl.Buffered(1)) if single_buffer_x else {}
    in_specs = [pl.BlockSpec((tm, k), lambda i, j: (i, 0), **x_mode),
                pl.BlockSpec((None, k, tn), lambda i, j: (layer, 0, j))]
    args = [x, w]
    body = _mm_kernel
    if residual is not None:
        in_specs.append(pl.BlockSpec((tm, tn), lambda i, j: (i, j)))
        args.append(residual)
        body = _mm_res_kernel
    return pl.pallas_call(
        body,
        grid=(m // tm, n // tn),
        in_specs=in_specs,
        out_specs=pl.BlockSpec((tm, tn), lambda i, j: (i, j)),
        out_shape=jax.ShapeDtypeStruct((m, n), out_dtype),
        compiler_params=_params("parallel", "arbitrary"),
        name=name,
    )(*args)


def _gate_kernel(x_ref, w_ref, b_ref, o_ref, carry_ref, *, ts, heads):
    c = pl.program_id(1)

    @pl.when(c == 0)
    def _():
        carry_ref[...] = jnp.zeros_like(carry_ref)

    wt = w_ref[...].T[:heads].astype(BF16)
    f = lax.dot_general(wt, x_ref[...], _NT,
                        preferred_element_type=F32) + b_ref[...]
    s = jnp.minimum(f, 0.0) - jnp.log1p(jnp.exp(-jnp.abs(f)))
    lane = lax.broadcasted_iota(jnp.int32, s.shape, 1)
    shift = 1
    while shift < ts:
        s = s + jnp.where(lane >= shift, pltpu.roll(s, shift, axis=1), 0.0)
        shift *= 2
    s = s + carry_ref[...]
    o_ref[0] = s
    carry_ref[...] = s[:, ts - 1:ts]


def _fox_gate(hn, w_in, layer, gate_col, b, *, batch, seq, ts=1024):
    heads = b.shape[0]
    d = hn.shape[1]
    assert seq % ts == 0 and heads % SUBLANE == 0 and heads <= LANE, (seq, ts, heads)
    assert gate_col % LANE == 0 and gate_col + heads <= w_in.shape[2], (gate_col, w_in.shape)
    nchunk = seq // ts
    return pl.pallas_call(
        functools.partial(_gate_kernel, ts=ts, heads=heads),
        grid=(batch, nchunk),
        in_specs=[pl.BlockSpec((ts, d), lambda bi, c: (bi * nchunk + c, 0)),
                  pl.BlockSpec((None, d, LANE), lambda bi, c: (layer, 0, gate_col // LANE)),
                  pl.BlockSpec((heads, 1), lambda bi, c: (0, 0))],
        out_specs=pl.BlockSpec((1, heads, ts), lambda bi, c: (bi, 0, c)),
        out_shape=jax.ShapeDtypeStruct((batch, heads, seq), F32),
        scratch_shapes=[pltpu.VMEM((heads, 1), F32)],
        compiler_params=_params("parallel", "arbitrary"),
        name="fox_gate",
    )(hn, w_in, b.reshape(heads, 1))


def _fox_kernel(q_ref, k_ref, v_ref, cq_ref, ck_ref, o_ref, cs_ref, m_ref, l_ref, acc_ref,
                *, tq, nheads, scale):
    i = pl.program_id(2)
    d = FOX_HEAD_DIM
    nchunk = ck_ref.shape[2]

    @pl.when(i == 0)
    def _():
        r = lax.broadcasted_iota(jnp.int32, (LANE, LANE), 0)
        c = lax.broadcasted_iota(jnp.int32, (LANE, LANE), 1)
        for hh in range(nheads):
            def chunk(t, carry, hh=hh):
                row = ck_ref[0, hh, pl.ds(t, 1), :] * LOG2E
                col = jnp.sum(jnp.where(r == c, row, 0.0), axis=1, keepdims=True)
                cs_ref[hh, pl.ds(pl.multiple_of(t * LANE, LANE), LANE), :] = (
                    jnp.broadcast_to(col, (LANE, LANE)))
                return carry
            lax.fori_loop(0, nchunk, chunk, 0, unroll=8)

    m_ref[...] = jnp.full_like(m_ref, -jnp.inf)
    l_ref[...] = jnp.zeros_like(l_ref)
    acc_ref[...] = jnp.zeros_like(acc_ref)

    def step(j, masked):
        start = pl.multiple_of(j * tq, tq)
        for hh in range(nheads):
            cols = slice(hh * d, (hh + 1) * d)
            kj = k_ref[pl.ds(start, tq), cols]
            vj = v_ref[pl.ds(start, tq), cols]
            s = lax.dot_general(kj, q_ref[:, cols], _NT,
                                preferred_element_type=F32)
            ct = cq_ref[0, hh:hh + 1, :] * LOG2E
            cs = jnp.tile(cs_ref[hh, pl.ds(start, tq), :], (1, tq // LANE))
            y = s * (scale * LOG2E) - cs
            if masked:
                r = lax.broadcasted_iota(jnp.int32, y.shape, 0)
                c = lax.broadcasted_iota(jnp.int32, y.shape, 1)
                y = jnp.where(c >= r, y, NEG)
            m_prev = m_ref[hh]
            m_new = jnp.maximum(m_prev, jnp.max(y, axis=0, keepdims=True) + ct)
            alpha = jnp.exp2(m_prev - m_new)
            p = jnp.exp2(y - (m_new - ct))
            l_ref[hh] = alpha * l_ref[hh] + jnp.sum(p, axis=0, keepdims=True)
            pv = lax.dot_general(vj, p.astype(BF16), (((0,), (0,)), ((), ())),
                                 preferred_element_type=F32)
            acc_ref[hh] = alpha * acc_ref[hh] + pv
            m_ref[hh] = m_new

    def body(j, carry):
        step(j, masked=False)
        return carry

    lax.fori_loop(0, i, body, 0)
    step(i, masked=True)
    for hh in range(nheads):
        out = acc_ref[hh] / l_ref[hh]
        o_ref[:, hh * d:(hh + 1) * d] = out.T.astype(o_ref.dtype)


def _fox_attention(qkv, cum, *, batch, seq, heads, tq=512, nheads=2):
    d = FOX_HEAD_DIM
    assert seq % tq == 0 and tq % LANE == 0 and heads % nheads == 0, (seq, tq, heads)
    assert qkv.shape == (batch * seq, 3 * heads * d), qkv.shape
    nq = seq // tq
    groups = heads // nheads
    w = nheads * d
    cum_q = cum.reshape(batch * groups, nheads, seq)
    cum_k = cum.reshape(batch * groups, nheads, seq // LANE, LANE)
    return pl.pallas_call(
        functools.partial(_fox_kernel, tq=tq, nheads=nheads, scale=d ** -0.5),
        grid=(batch, groups, nq),
        in_specs=[
            pl.BlockSpec((tq, w), lambda b, g, i: (b * nq + i, g)),
            pl.BlockSpec((seq, w), lambda b, g, i: (b, groups + g)),
            pl.BlockSpec((seq, w), lambda b, g, i: (b, 2 * groups + g)),
            pl.BlockSpec((1, nheads, tq), lambda b, g, i: (b * groups + g, 0, i)),
            pl.BlockSpec((1, nheads, seq // LANE, LANE),
                         lambda b, g, i: (b * groups + g, 0, 0, 0)),
        ],
        out_specs=pl.BlockSpec((tq, w), lambda b, g, i: (b * nq + i, g)),
        out_shape=jax.ShapeDtypeStruct((batch * seq, heads * d), BF16),
        scratch_shapes=[pltpu.VMEM((nheads, seq, LANE), F32),
                        pltpu.VMEM((nheads, 1, tq), F32),
                        pltpu.VMEM((nheads, 1, tq), F32),
                        pltpu.VMEM((nheads, d, tq), F32)],
        compiler_params=_params("parallel", "parallel", "arbitrary"),
        name="fox_attention",
    )(qkv, qkv, qkv, cum_q, cum_k)


def _swa_kernel(slope_ref, sink_ref, q_ref, kp_ref, kc_ref, vp_ref, vc_ref, o_ref,
                *, scale):
    n = pl.program_id(1)
    pair = pl.program_id(2)
    band = 2 * Q_BLOCK
    half = SWA_HEAD_DIM

    kband = jnp.concatenate([kp_ref[...], kc_ref[...]], axis=0).astype(F32)
    vband = jnp.concatenate([vp_ref[...], vc_ref[...]], axis=0).astype(F32)
    kroll = pltpu.roll(kband, half, axis=1)
    vroll = pltpu.roll(vband, half, axis=1)
    lo = lax.broadcasted_iota(jnp.int32, kband.shape, 1) < half

    r = lax.broadcasted_iota(jnp.int32, (Q_BLOCK, band), 0)
    c = lax.broadcasted_iota(jnp.int32, (Q_BLOCK, band), 1)
    dist = jnp.abs(r - (c - Q_BLOCK)).astype(F32)
    qc = r // CHUNK
    kc = c // CHUNK - WINDOW_CHUNKS
    valid = (kc <= qc) & (kc >= qc - WINDOW_CHUNKS) & ((c >= Q_BLOCK) | (n > 0))

    for cc in range(2):
        k_src, k_alt = (kband, kroll) if cc == 0 else (kroll, kband)
        v_src, v_alt = (vband, vroll) if cc == 0 else (vroll, vband)
        k_even = jnp.where(lo, k_src, 0.0).astype(BF16)
        k_odd = jnp.where(lo, 0.0, k_alt).astype(BF16)
        v_even = jnp.where(lo, v_src, 0.0).astype(BF16)
        v_odd = jnp.where(lo, 0.0, v_alt).astype(BF16)

        npair = SWA_GROUP // 2
        qs = jnp.concatenate(
            [q_ref[:, (cc * npair + t) * LANE:(cc * npair + t + 1) * LANE]
             for t in range(npair)], axis=0)
        scores = (lax.dot_general(qs, k_even, _NT, preferred_element_type=F32),
                  lax.dot_general(qs, k_odd, _NT, preferred_element_type=F32))
        probs = ([], [])
        for t in range(npair):
            for par in range(2):
                head = (2 * pair + cc) * SWA_GROUP + 2 * t + par
                slope = slope_ref[head]
                sink = sink_ref[head]
                s = scores[par][t * Q_BLOCK:(t + 1) * Q_BLOCK] * scale - slope * dist
                s = jnp.where(valid, s, NEG)
                m = jnp.maximum(jnp.max(s, axis=1, keepdims=True), sink)
                e = jnp.exp(s - m)
                denom = jnp.sum(e, axis=1, keepdims=True) + jnp.exp(sink - m)
                probs[par].append((e / denom).astype(BF16))
        p_even = jnp.concatenate(probs[0], axis=0)
        p_odd = jnp.concatenate(probs[1], axis=0)
        out = (jnp.dot(p_even, v_even, preferred_element_type=F32)
               + jnp.dot(p_odd, v_odd, preferred_element_type=F32))
        for t in range(npair):
            o_ref[:, (cc * npair + t) * LANE:(cc * npair + t + 1) * LANE] = (
                out[t * Q_BLOCK:(t + 1) * Q_BLOCK].astype(o_ref.dtype))


def _swa_attention(proj, slopes, sinks, *, batch, seq, d_model):
    nb = seq // Q_BLOCK
    kv_heads = d_model // SWA_HEAD_DIM // SWA_GROUP
    npairs = kv_heads // 2
    assert seq % Q_BLOCK == 0 and kv_heads % 2 == 0, (seq, kv_heads)
    assert proj.shape == (batch * seq, d_model + 2 * kv_heads * SWA_HEAD_DIM), proj.shape
    qw = 2 * SWA_GROUP * SWA_HEAD_DIM
    k0 = d_model // LANE
    v0 = k0 + npairs
    prev = lambda b, n, p: b * nb + jnp.maximum(n - 1, 0)
    smem = pl.BlockSpec(memory_space=pltpu.SMEM)
    return pl.pallas_call(
        functools.partial(_swa_kernel, scale=SWA_HEAD_DIM ** -0.5),
        grid=(batch, nb, npairs),
        in_specs=[
            smem, smem,
            pl.BlockSpec((Q_BLOCK, qw), lambda b, n, p: (b * nb + n, p)),
            pl.BlockSpec((Q_BLOCK, LANE), lambda b, n, p: (prev(b, n, p), k0 + p)),
            pl.BlockSpec((Q_BLOCK, LANE), lambda b, n, p: (b * nb + n, k0 + p)),
            pl.BlockSpec((Q_BLOCK, LANE), lambda b, n, p: (prev(b, n, p), v0 + p)),
            pl.BlockSpec((Q_BLOCK, LANE), lambda b, n, p: (b * nb + n, v0 + p)),
        ],
        out_specs=pl.BlockSpec((Q_BLOCK, qw), lambda b, n, p: (b * nb + n, p)),
        out_shape=jax.ShapeDtypeStruct((batch * seq, d_model), BF16),
        compiler_params=_params("parallel", "parallel", "arbitrary"),
        name="swa_attention",
    )(slopes, sinks, proj, proj, proj, proj, proj)


def _up_kernel(x_ref, wg_ref, wv_ref, cwg_ref, cwv_ref, cbg_ref, cbv_ref, o_ref,
               ug_ref, uv_ref, tailg_ref, tailv_ref, *, tm, seq_tiles):
    i = pl.program_id(0)
    j = pl.program_id(1)
    x = x_ref[...]
    seq_start = (i % seq_tiles) == 0

    def conv(w_ref, u_ref, tail_ref, cw_ref, cb_ref):
        u_ref[0:SUBLANE, :] = jnp.where(seq_start, 0.0, tail_ref[j])
        u_ref[SUBLANE:SUBLANE + tm, :] = jnp.dot(
            x, w_ref[...].astype(BF16), preferred_element_type=F32)
        tail_ref[j] = u_ref[tm:tm + SUBLANE, :]
        cw = cw_ref[...]
        out = cb_ref[...]
        for tap in range(CONV_WIDTH):
            off = SUBLANE - (CONV_WIDTH - 1) + tap
            out = out + cw[tap:tap + 1, :] * u_ref[off:off + tm, :]
        return out

    gate = conv(wg_ref, ug_ref, tailg_ref, cwg_ref, cbg_ref)
    val = conv(wv_ref, uv_ref, tailv_ref, cwv_ref, cbv_ref)
    o_ref[...] = (gate * (1.0 / (1.0 + jnp.exp(-gate))) * val).astype(o_ref.dtype)


def _conv_ffn_up(hn, w_up, conv_w, conv_b, layer, *, seq, tm=1024, tn=256):
    m, d = hn.shape
    f = w_up.shape[2] // 2
    assert m % tm == 0 and seq % tm == 0 and f % tn == 0, (m, seq, f, tm, tn)
    nj = f // tn
    conv_b = conv_b.reshape(conv_b.shape[0], 1, 2 * f)
    gate = lambda i, j: (layer, 0, j)
    val = lambda i, j: (layer, 0, nj + j)
    slot = pltpu.VMEM((tm + SUBLANE, tn), F32)
    tail = pltpu.VMEM((nj, SUBLANE, tn), F32)
    return pl.pallas_call(
        functools.partial(_up_kernel, tm=tm, seq_tiles=seq // tm),
        grid=(m // tm, nj),
        in_specs=[
            pl.BlockSpec((tm, d), lambda i, j: (i, 0)),
            pl.BlockSpec((None, d, tn), gate),
            pl.BlockSpec((None, d, tn), val),
            pl.BlockSpec((None, CONV_WIDTH, tn), gate),
            pl.BlockSpec((None, CONV_WIDTH, tn), val),
            pl.BlockSpec((None, 1, tn), gate),
            pl.BlockSpec((None, 1, tn), val),
        ],
        out_specs=pl.BlockSpec((tm, tn), lambda i, j: (i, j)),
        out_shape=jax.ShapeDtypeStruct((m, f), BF16),
        scratch_shapes=[slot, slot, tail, tail],
        compiler_params=_params("arbitrary", "arbitrary"),
        name="conv_ffn_up",
    )(hn, w_up, w_up, conv_w, conv_w, conv_b, conv_b)


def kernel(x, attn_norm_g, fox_w_in, fox_b_f, fox_w_o, swa_w_in, swa_sinks, swa_w_o,
           ffn_norm_g, ffn_w_up, ffn_conv_w, ffn_conv_b, ffn_w_down, final_norm_g):
    batch, seq, d_model = x.shape
    depth = attn_norm_g.shape[0]
    fox_heads = fox_b_f.shape[1]
    swa_heads = swa_sinks.shape[1]
    slopes = jnp.exp2(-8.0 * jnp.arange(1, swa_heads + 1, dtype=F32) / swa_heads)

    h = x.reshape(batch * seq, d_model)
    for layer in range(depth):
        inst = layer // 2
        hn = _rmsnorm(h, attn_norm_g[layer], BF16)
        if layer % 2 == 0:
            qkv = _mm(hn, fox_w_in, inst, 3 * d_model, BF16, tm=1024, tn=512, name="fox_qkv")
            cum = _fox_gate(hn, fox_w_in, inst, 3 * d_model, fox_b_f[inst],
                            batch=batch, seq=seq)
            mix = _fox_attention(qkv, cum, batch=batch, seq=seq, heads=fox_heads)
            w_o = fox_w_o
        else:
            proj = _mm(hn, swa_w_in, inst, swa_w_in.shape[2], BF16,
                       tm=1024, tn=512, name="swa_qkv")
            mix = _swa_attention(proj, slopes, swa_sinks[inst],
                                 batch=batch, seq=seq, d_model=d_model)
            w_o = swa_w_o
        h = _mm(mix, w_o, inst, d_model, F32, tm=1024, tn=512, residual=h, name="attn_out")
        hn = _rmsnorm(h, ffn_norm_g[layer], BF16)
        act = _conv_ffn_up(hn, ffn_w_up, ffn_conv_w, ffn_conv_b, layer, seq=seq)
        h = _mm(act, ffn_w_down, layer, d_model, F32, tm=1024, tn=256,
                residual=h, single_buffer_x=True, name="ffn_down")
    out = _rmsnorm(h, final_norm_g, F32)
    return out.reshape(batch, seq, d_model)
```

```python
import functools

import jax
import jax.numpy as jnp
from jax import lax
from jax.experimental import pallas as pl
from jax.experimental.pallas import tpu as pltpu

LANE = 128
SUBLANE = 8
VMEM_LIMIT_BYTES = 56 * 1024 * 1024

FOX_HEAD_DIM = 128
SWA_HEAD_DIM = 64
SWA_GROUP = 8
Q_BLOCK = 128
CHUNK = 64
WINDOW_CHUNKS = 2
CONV_WIDTH = 3
EPS = 1e-6
NEG = -1e30
LOG2E = 1.4426950408889634

BF16 = jnp.bfloat16
F32 = jnp.float32

_NT = (((1,), (1,)), ((), ()))


def _params(*semantics):
    return pltpu.CompilerParams(
        dimension_semantics=semantics, vmem_limit_bytes=VMEM_LIMIT_BYTES)


def _rmsnorm_kernel(x_ref, g_ref, o_ref):
    x = x_ref[...]
    ms = jnp.mean(x * x, axis=-1, keepdims=True)
    o_ref[...] = (x * lax.rsqrt(ms + EPS) * g_ref[...]).astype(o_ref.dtype)


def _rmsnorm(x, g, out_dtype, *, rows=256):
    m, d = x.shape
    assert m % rows == 0 and d % LANE == 0, (x.shape, rows)
    return pl.pallas_call(
        _rmsnorm_kernel,
        grid=(m // rows,),
        in_specs=[pl.BlockSpec((rows, d), lambda i: (i, 0)),
                  pl.BlockSpec((1, d), lambda i: (0, 0))],
        out_specs=pl.BlockSpec((rows, d), lambda i: (i, 0)),
        out_shape=jax.ShapeDtypeStruct((m, d), out_dtype),
        compiler_params=_params("parallel"),
        name="rmsnorm",
    )(x, g.reshape(1, d))


def _mm_kernel(*refs, transposed_w, has_residual):
    x_ref, w_ref = refs[:2]
    o_ref = refs[-1]
    w = w_ref[...].astype(BF16)
    if transposed_w:
        acc = lax.dot_general(x_ref[...], w, _NT, preferred_element_type=F32)
    else:
        acc = jnp.dot(x_ref[...], w, preferred_element_type=F32)
    if has_residual:
        acc = refs[2][...] + acc
    o_ref[...] = acc.astype(o_ref.dtype)


def _mm(x, w, layer, n, out_dtype, *, tm, tn, residual=None, transposed_w=False,
        single_buffer_x=False, name="mm"):
    m, k = x.shape
    k_axis, n_axis = (2, 1) if transposed_w else (1, 2)
    assert w.shape[k_axis] == k and w.shape[n_axis] >= n, (x.shape, w.shape, n)
    assert m % tm == 0 and n % tn == 0, (m, n, tm, tn)
    x_mode = dict(pipeline_mode=pl.Buffered(1)) if single_buffer_x else {}
    if transposed_w:
        w_spec = pl.BlockSpec((None, tn, k), lambda i, j: (layer, j, 0))
    else:
        w_spec = pl.BlockSpec((None, k, tn), lambda i, j: (layer, 0, j))
    in_specs = [pl.BlockSpec((tm, k), lambda i, j: (i, 0), **x_mode), w_spec]
    args = [x, w]
    if residual is not None:
        in_specs.append(pl.BlockSpec((tm, tn), lambda i, j: (i, j)))
        args.append(residual)
    return pl.pallas_call(
        functools.partial(_mm_kernel, transposed_w=transposed_w,
                          has_residual=residual is not None),
        grid=(m // tm, n // tn),
        in_specs=in_specs,
        out_specs=pl.BlockSpec((tm, tn), lambda i, j: (i, j)),
        out_shape=jax.ShapeDtypeStruct((m, n), out_dtype),
        compiler_params=_params("parallel", "arbitrary"),
        name=name,
    )(*args)


def _gate_kernel(x_ref, wt_ref, b_ref, o_ref, carry_ref, *, ts):
    c = pl.program_id(1)

    @pl.when(c == 0)
    def _():
        carry_ref[...] = jnp.zeros_like(carry_ref)

    f = lax.dot_general(wt_ref[...].astype(BF16), x_ref[...], _NT,
                        preferred_element_type=F32) + b_ref[...]
    s = jnp.minimum(f, 0.0) - jnp.log1p(jnp.exp(-jnp.abs(f)))
    lane = lax.broadcasted_iota(jnp.int32, s.shape, 1)
    shift = 1
    while shift < ts:
        s = s + jnp.where(lane >= shift, pltpu.roll(s, shift, axis=1), 0.0)
        shift *= 2
    s = s + carry_ref[...]
    o_ref[0] = s
    carry_ref[...] = s[:, ts - 1:ts]


def _fox_gate(hn, w_in_t, layer, gate_row, b, *, batch, seq, ts=1024):
    heads = b.shape[0]
    d = hn.shape[1]
    assert seq % ts == 0 and heads % SUBLANE == 0, (seq, ts, heads)
    assert gate_row % heads == 0 and gate_row + heads <= w_in_t.shape[1], (gate_row, w_in_t.shape)
    nchunk = seq // ts
    return pl.pallas_call(
        functools.partial(_gate_kernel, ts=ts),
        grid=(batch, nchunk),
        in_specs=[pl.BlockSpec((ts, d), lambda bi, c: (bi * nchunk + c, 0)),
                  pl.BlockSpec((None, heads, d), lambda bi, c: (layer, gate_row // heads, 0)),
                  pl.BlockSpec((heads, 1), lambda bi, c: (0, 0))],
        out_specs=pl.BlockSpec((1, heads, ts), lambda bi, c: (bi, 0, c)),
        out_shape=jax.ShapeDtypeStruct((batch, heads, seq), F32),
        scratch_shapes=[pltpu.VMEM((heads, 1), F32)],
        compiler_params=_params("parallel", "arbitrary"),
        name="fox_gate",
    )(hn, w_in_t, b.reshape(heads, 1))


def _fox_kernel(q_ref, k_ref, v_ref, cq_ref, ck_ref, o_ref, cs_ref, m_ref, l_ref, acc_ref,
                *, tq, nheads, scale):
    i = pl.program_id(2)
    d = FOX_HEAD_DIM
    nchunk = ck_ref.shape[2]

    @pl.when(i == 0)
    def _():
        r = lax.broadcasted_iota(jnp.int32, (LANE, LANE), 0)
        c = lax.broadcasted_iota(jnp.int32, (LANE, LANE), 1)
        for hh in range(nheads):
            def chunk(t, carry, hh=hh):
                row = ck_ref[0, hh, pl.ds(t, 1), :] * LOG2E
                col = jnp.sum(jnp.where(r == c, row, 0.0), axis=1, keepdims=True)
                cs_ref[hh, pl.ds(pl.multiple_of(t * LANE, LANE), LANE), :] = (
                    jnp.broadcast_to(col, (LANE, LANE)))
                return carry
            lax.fori_loop(0, nchunk, chunk, 0, unroll=8)

    m_ref[...] = jnp.full_like(m_ref, -jnp.inf)
    l_ref[...] = jnp.zeros_like(l_ref)
    acc_ref[...] = jnp.zeros_like(acc_ref)

    def step(j, masked):
        start = pl.multiple_of(j * tq, tq)
        heads = range(nheads)
        cols = [slice(hh * d, (hh + 1) * d) for hh in heads]
        scores = [lax.dot_general(k_ref[pl.ds(start, tq), cols[hh]], q_ref[:, cols[hh]], _NT,
                                  preferred_element_type=F32) for hh in heads]
        ys, shifts, alphas = [], [], []
        for hh in heads:
            ct = cq_ref[0, hh:hh + 1, :] * LOG2E
            cs = jnp.tile(cs_ref[hh, pl.ds(start, tq), :], (1, tq // LANE))
            y = scores[hh] * (scale * LOG2E) - cs
            if masked:
                r = lax.broadcasted_iota(jnp.int32, y.shape, 0)
                c = lax.broadcasted_iota(jnp.int32, y.shape, 1)
                y = jnp.where(c >= r, y, NEG)
            m_prev = m_ref[hh]
            m_new = jnp.maximum(m_prev, jnp.max(y, axis=0, keepdims=True) + ct)
            m_ref[hh] = m_new
            ys.append(y)
            shifts.append(m_new - ct)
            alphas.append(jnp.exp2(m_prev - m_new))
        for hh in heads:
            p = jnp.exp2(ys[hh] - shifts[hh])
            l_ref[hh] = alphas[hh] * l_ref[hh] + jnp.sum(p, axis=0, keepdims=True)
            pv = lax.dot_general(v_ref[pl.ds(start, tq), cols[hh]], p.astype(BF16),
                                 (((0,), (0,)), ((), ())),
                                 preferred_element_type=F32)
            acc_ref[hh] = alphas[hh] * acc_ref[hh] + pv

    def body(j, carry):
        step(j, masked=False)
        return carry

    lax.fori_loop(0, i, body, 0)
    step(i, masked=True)
    for hh in range(nheads):
        out = acc_ref[hh] / l_ref[hh]
        o_ref[:, hh * d:(hh + 1) * d] = out.T.astype(o_ref.dtype)


def _fox_attention(qkv, cum, *, batch, seq, heads, tq=512, nheads=4):
    d = FOX_HEAD_DIM
    assert seq % tq == 0 and tq % LANE == 0 and heads % nheads == 0, (seq, tq, heads)
    assert qkv.shape == (batch * seq, 3 * heads * d), qkv.shape
    nq = seq // tq
    groups = heads // nheads
    w = nheads * d
    cum_q = cum.reshape(batch * groups, nheads, seq)
    cum_k = cum.reshape(batch * groups, nheads, seq // LANE, LANE)
    return pl.pallas_call(
        functools.partial(_fox_kernel, tq=tq, nheads=nheads, scale=d ** -0.5),
        grid=(batch, groups, nq),
        in_specs=[
            pl.BlockSpec((tq, w), lambda b, g, i: (b * nq + i, g)),
            pl.BlockSpec((seq, w), lambda b, g, i: (b, groups + g)),
            pl.BlockSpec((seq, w), lambda b, g, i: (b, 2 * groups + g)),
            pl.BlockSpec((1, nheads, tq), lambda b, g, i: (b * groups + g, 0, i)),
            pl.BlockSpec((1, nheads, seq // LANE, LANE),
                         lambda b, g, i: (b * groups + g, 0, 0, 0)),
        ],
        out_specs=pl.BlockSpec((tq, w), lambda b, g, i: (b * nq + i, g)),
        out_shape=jax.ShapeDtypeStruct((batch * seq, heads * d), BF16),
        scratch_shapes=[pltpu.VMEM((nheads, seq, LANE), F32),
                        pltpu.VMEM((nheads, 1, tq), F32),
                        pltpu.VMEM((nheads, 1, tq), F32),
                        pltpu.VMEM((nheads, d, tq), F32)],
        compiler_params=_params("parallel", "parallel", "arbitrary"),
        name="fox_attention",
    )(qkv, qkv, qkv, cum_q, cum_k)


def _swa_kernel(slope_ref, sink_ref, q_ref, kp_ref, kc_ref, vp_ref, vc_ref, o_ref,
                *, scale):
    n = pl.program_id(1)
    pair = pl.program_id(2)
    band = 2 * Q_BLOCK
    half = SWA_HEAD_DIM

    kband = jnp.concatenate([kp_ref[...], kc_ref[...]], axis=0).astype(F32)
    vband = jnp.concatenate([vp_ref[...], vc_ref[...]], axis=0).astype(F32)
    kroll = pltpu.roll(kband, half, axis=1)
    vroll = pltpu.roll(vband, half, axis=1)
    lo = lax.broadcasted_iota(jnp.int32, kband.shape, 1) < half

    r = lax.broadcasted_iota(jnp.int32, (Q_BLOCK, band), 0)
    c = lax.broadcasted_iota(jnp.int32, (Q_BLOCK, band), 1)
    dist = jnp.abs(r - (c - Q_BLOCK)).astype(F32)
    qc = r // CHUNK
    kc = c // CHUNK - WINDOW_CHUNKS
    valid = (kc <= qc) & (kc >= qc - WINDOW_CHUNKS) & ((c >= Q_BLOCK) | (n > 0))

    for cc in range(2):
        k_src, k_alt = (kband, kroll) if cc == 0 else (kroll, kband)
        v_src, v_alt = (vband, vroll) if cc == 0 else (vroll, vband)
        k_even = jnp.where(lo, k_src, 0.0).astype(BF16)
        k_odd = jnp.where(lo, 0.0, k_alt).astype(BF16)
        v_even = jnp.where(lo, v_src, 0.0).astype(BF16)
        v_odd = jnp.where(lo, 0.0, v_alt).astype(BF16)

        npair = SWA_GROUP // 2
        qs = jnp.concatenate(
            [q_ref[:, (cc * npair + t) * LANE:(cc * npair + t + 1) * LANE]
             for t in range(npair)], axis=0)
        scores = (lax.dot_general(qs, k_even, _NT, preferred_element_type=F32),
                  lax.dot_general(qs, k_odd, _NT, preferred_element_type=F32))
        probs = ([], [])
        for t in range(npair):
            for par in range(2):
                head = (2 * pair + cc) * SWA_GROUP + 2 * t + par
                slope = slope_ref[head]
                sink = sink_ref[head]
                s = scores[par][t * Q_BLOCK:(t + 1) * Q_BLOCK] * scale - slope * dist
                s = jnp.where(valid, s, NEG)
                m = jnp.maximum(jnp.max(s, axis=1, keepdims=True), sink)
                e = jnp.exp(s - m)
                denom = jnp.sum(e, axis=1, keepdims=True) + jnp.exp(sink - m)
                probs[par].append((e / denom).astype(BF16))
        p_even = jnp.concatenate(probs[0], axis=0)
        p_odd = jnp.concatenate(probs[1], axis=0)
        out = (jnp.dot(p_even, v_even, preferred_element_type=F32)
               + jnp.dot(p_odd, v_odd, preferred_element_type=F32))
        for t in range(npair):
            o_ref[:, (cc * npair + t) * LANE:(cc * npair + t + 1) * LANE] = (
                out[t * Q_BLOCK:(t + 1) * Q_BLOCK].astype(o_ref.dtype))


def _swa_attention(proj, slopes, sinks, *, batch, seq, d_model):
    nb = seq // Q_BLOCK
    kv_heads = d_model // SWA_HEAD_DIM // SWA_GROUP
    npairs = kv_heads // 2
    assert seq % Q_BLOCK == 0 and kv_heads % 2 == 0, (seq, kv_heads)
    assert proj.shape == (batch * seq, d_model + 2 * kv_heads * SWA_HEAD_DIM), proj.shape
    qw = 2 * SWA_GROUP * SWA_HEAD_DIM
    k0 = d_model // LANE
    v0 = k0 + npairs
    prev = lambda b, n, p: b * nb + jnp.maximum(n - 1, 0)
    smem = pl.BlockSpec(memory_space=pltpu.SMEM)
    return pl.pallas_call(
        functools.partial(_swa_kernel, scale=SWA_HEAD_DIM ** -0.5),
        grid=(batch, nb, npairs),
        in_specs=[
            smem, smem,
            pl.BlockSpec((Q_BLOCK, qw), lambda b, n, p: (b * nb + n, p)),
            pl.BlockSpec((Q_BLOCK, LANE), lambda b, n, p: (prev(b, n, p), k0 + p)),
            pl.BlockSpec((Q_BLOCK, LANE), lambda b, n, p: (b * nb + n, k0 + p)),
            pl.BlockSpec((Q_BLOCK, LANE), lambda b, n, p: (prev(b, n, p), v0 + p)),
            pl.BlockSpec((Q_BLOCK, LANE), lambda b, n, p: (b * nb + n, v0 + p)),
        ],
        out_specs=pl.BlockSpec((Q_BLOCK, qw), lambda b, n, p: (b * nb + n, p)),
        out_shape=jax.ShapeDtypeStruct((batch * seq, d_model), BF16),
        compiler_params=_params("parallel", "parallel", "arbitrary"),
        name="swa_attention",
    )(slopes, sinks, proj, proj, proj, proj, proj)


def _up_kernel(x_ref, wg_ref, wv_ref, cwg_ref, cwv_ref, cbg_ref, cbv_ref, o_ref,
               ug_ref, uv_ref, tailg_ref, tailv_ref, *, tm, seq_tiles):
    i = pl.program_id(0)
    j = pl.program_id(1)
    x = x_ref[...]
    seq_start = (i % seq_tiles) == 0

    def conv(w_ref, u_ref, tail_ref, cw_ref, cb_ref):
        u_ref[0:SUBLANE, :] = jnp.where(seq_start, 0.0, tail_ref[j])
        u_ref[SUBLANE:SUBLANE + tm, :] = jnp.dot(
            x, w_ref[...].astype(BF16), preferred_element_type=F32)
        tail_ref[j] = u_ref[tm:tm + SUBLANE, :]
        cw = cw_ref[...]
        out = cb_ref[...]
        for tap in range(CONV_WIDTH):
            off = SUBLANE - (CONV_WIDTH - 1) + tap
            out = out + cw[tap:tap + 1, :] * u_ref[off:off + tm, :]
        return out

    gate = conv(wg_ref, ug_ref, tailg_ref, cwg_ref, cbg_ref)
    val = conv(wv_ref, uv_ref, tailv_ref, cwv_ref, cbv_ref)
    o_ref[...] = (gate * (1.0 / (1.0 + jnp.exp(-gate))) * val).astype(o_ref.dtype)


def _conv_ffn_up(hn, w_up, conv_w, conv_b, layer, *, seq, tm=1024, tn=256):
    m, d = hn.shape
    f = w_up.shape[2] // 2
    assert m % tm == 0 and seq % tm == 0 and f % tn == 0, (m, seq, f, tm, tn)
    nj = f // tn
    conv_b = conv_b.reshape(conv_b.shape[0], 1, 2 * f)
    gate = lambda i, j: (layer, 0, j)
    val = lambda i, j: (layer, 0, nj + j)
    slot = pltpu.VMEM((tm + SUBLANE, tn), F32)
    tail = pltpu.VMEM((nj, SUBLANE, tn), F32)
    return pl.pallas_call(
        functools.partial(_up_kernel, tm=tm, seq_tiles=seq // tm),
        grid=(m // tm, nj),
        in_specs=[
            pl.BlockSpec((tm, d), lambda i, j: (i, 0)),
            pl.BlockSpec((None, d, tn), gate),
            pl.BlockSpec((None, d, tn), val),
            pl.BlockSpec((None, CONV_WIDTH, tn), gate),
            pl.BlockSpec((None, CONV_WIDTH, tn), val),
            pl.BlockSpec((None, 1, tn), gate),
            pl.BlockSpec((None, 1, tn), val),
        ],
        out_specs=pl.BlockSpec((tm, tn), lambda i, j: (i, j)),
        out_shape=jax.ShapeDtypeStruct((m, f), BF16),
        scratch_shapes=[slot, slot, tail, tail],
        compiler_params=_params("arbitrary", "arbitrary"),
        name="conv_ffn_up",
    )(hn, w_up, w_up, conv_w, conv_w, conv_b, conv_b)


def kernel(x, attn_norm_g, fox_w_in, fox_b_f, fox_w_o, swa_w_in, swa_sinks, swa_w_o,
           ffn_norm_g, ffn_w_up, ffn_conv_w, ffn_conv_b, ffn_w_down, final_norm_g):
    batch, seq, d_model = x.shape
    depth = attn_norm_g.shape[0]
    fox_heads = fox_b_f.shape[1]
    swa_heads = swa_sinks.shape[1]
    slopes = jnp.exp2(-8.0 * jnp.arange(1, swa_heads + 1, dtype=F32) / swa_heads)

    fox_w_in_t = jnp.swapaxes(fox_w_in, 1, 2)

    h = x.reshape(batch * seq, d_model)
    for layer in range(depth):
        inst = layer // 2
        hn = _rmsnorm(h, attn_norm_g[layer], BF16)
        if layer % 2 == 0:
            qkv = _mm(hn, fox_w_in_t, inst, 3 * d_model, BF16, tm=1024, tn=512,
                      transposed_w=True, name="fox_qkv")
            cum = _fox_gate(hn, fox_w_in_t, inst, 3 * d_model, fox_b_f[inst],
                            batch=batch, seq=seq)
            mix = _fox_attention(qkv, cum, batch=batch, seq=seq, heads=fox_heads)
            w_o = fox_w_o
        else:
            proj = _mm(hn, swa_w_in, inst, swa_w_in.shape[2], BF16,
                       tm=1024, tn=512, name="swa_qkv")
            mix = _swa_attention(proj, slopes, swa_sinks[inst],
                                 batch=batch, seq=seq, d_model=d_model)
            w_o = swa_w_o
        h = _mm(mix, w_o, inst, d_model, F32, tm=1024, tn=512, residual=h, name="attn_out")
        hn = _rmsnorm(h, ffn_norm_g[layer], BF16)
        act = _conv_ffn_up(hn, ffn_w_up, ffn_conv_w, ffn_conv_b, layer, seq=seq)
        h = _mm(act, ffn_w_down, layer, d_model, F32, tm=1024, tn=256,
                residual=h, single_buffer_x=True, name="ffn_down")
    out = _rmsnorm(h, final_norm_g, F32)
    return out.reshape(batch, seq, d_model)
```

```python
import functools

import jax
import jax.numpy as jnp
from jax import lax
from jax.experimental import pallas as pl
from jax.experimental.pallas import tpu as pltpu

LANE = 128
SUBLANE = 8
VMEM_LIMIT_BYTES = 56 * 1024 * 1024

FOX_HEAD_DIM = 128
SWA_HEAD_DIM = 64
SWA_GROUP = 8
Q_BLOCK = 128
CHUNK = 64
WINDOW_CHUNKS = 2
CONV_WIDTH = 3
EPS = 1e-6
NEG = -1e30
MASKED_DIST = 1e30
LOG2E = 1.4426950408889634

BF16 = jnp.bfloat16
F32 = jnp.float32

_NT = (((1,), (1,)), ((), ()))


def _params(*semantics):
    return pltpu.CompilerParams(
        dimension_semantics=semantics, vmem_limit_bytes=VMEM_LIMIT_BYTES)


def _rmsnorm_kernel(x_ref, g_ref, o_ref):
    x = x_ref[...]
    ms = jnp.mean(x * x, axis=-1, keepdims=True)
    o_ref[...] = (x * lax.rsqrt(ms + EPS) * g_ref[...]).astype(o_ref.dtype)


def _rmsnorm(x, g, out_dtype, *, rows=256):
    m, d = x.shape
    assert m % rows == 0 and d % LANE == 0, (x.shape, rows)
    return pl.pallas_call(
        _rmsnorm_kernel,
        grid=(m // rows,),
        in_specs=[pl.BlockSpec((rows, d), lambda i: (i, 0)),
                  pl.BlockSpec((1, d), lambda i: (0, 0))],
        out_specs=pl.BlockSpec((rows, d), lambda i: (i, 0)),
        out_shape=jax.ShapeDtypeStruct((m, d), out_dtype),
        compiler_params=_params("parallel"),
        name="rmsnorm",
    )(x, g.reshape(1, d))


def _mm_kernel(*refs, transposed_w, has_residual):
    x_ref, w_ref = refs[:2]
    o_ref = refs[-1]
    w = w_ref[...].astype(BF16)
    if transposed_w:
        acc = lax.dot_general(x_ref[...], w, _NT, preferred_element_type=F32)
    else:
        acc = jnp.dot(x_ref[...], w, preferred_element_type=F32)
    if has_residual:
        acc = refs[2][...] + acc
    o_ref[...] = acc.astype(o_ref.dtype)


def _mm(x, w, layer, n, out_dtype, *, tm, tn, residual=None, transposed_w=False,
        single_buffer_x=False, name="mm"):
    m, k = x.shape
    k_axis, n_axis = (2, 1) if transposed_w else (1, 2)
    assert w.shape[k_axis] == k and w.shape[n_axis] >= n, (x.shape, w.shape, n)
    assert m % tm == 0 and n % tn == 0, (m, n, tm, tn)
    x_mode = dict(pipeline_mode=pl.Buffered(1)) if single_buffer_x else {}
    if transposed_w:
        w_spec = pl.BlockSpec((None, tn, k), lambda i, j: (layer, j, 0))
    else:
        w_spec = pl.BlockSpec((None, k, tn), lambda i, j: (layer, 0, j))
    in_specs = [pl.BlockSpec((tm, k), lambda i, j: (i, 0), **x_mode), w_spec]
    args = [x, w]
    if residual is not None:
        in_specs.append(pl.BlockSpec((tm, tn), lambda i, j: (i, j)))
        args.append(residual)
    return pl.pallas_call(
        functools.partial(_mm_kernel, transposed_w=transposed_w,
                          has_residual=residual is not None),
        grid=(m // tm, n // tn),
        in_specs=in_specs,
        out_specs=pl.BlockSpec((tm, tn), lambda i, j: (i, j)),
        out_shape=jax.ShapeDtypeStruct((m, n), out_dtype),
        compiler_params=_params("parallel", "arbitrary"),
        name=name,
    )(*args)


def _gate_kernel(x_ref, wt_ref, b_ref, o_ref, carry_ref, *, ts):
    c = pl.program_id(1)

    @pl.when(c == 0)
    def _():
        carry_ref[...] = jnp.zeros_like(carry_ref)

    f = lax.dot_general(wt_ref[...].astype(BF16), x_ref[...], _NT,
                        preferred_element_type=F32) + b_ref[...]
    s = jnp.minimum(f, 0.0) - jnp.log1p(jnp.exp(-jnp.abs(f)))
    lane = lax.broadcasted_iota(jnp.int32, s.shape, 1)
    shift = 1
    while shift < ts:
        s = s + jnp.where(lane >= shift, pltpu.roll(s, shift, axis=1), 0.0)
        shift *= 2
    s = s + carry_ref[...]
    o_ref[0] = s
    carry_ref[...] = s[:, ts - 1:ts]


def _fox_gate(hn, w_in_t, layer, gate_row, b, *, batch, seq, ts=1024):
    heads = b.shape[0]
    d = hn.shape[1]
    assert seq % ts == 0 and heads % SUBLANE == 0, (seq, ts, heads)
    assert gate_row % heads == 0 and gate_row + heads <= w_in_t.shape[1], (gate_row, w_in_t.shape)
    nchunk = seq // ts
    return pl.pallas_call(
        functools.partial(_gate_kernel, ts=ts),
        grid=(batch, nchunk),
        in_specs=[pl.BlockSpec((ts, d), lambda bi, c: (bi * nchunk + c, 0)),
                  pl.BlockSpec((None, heads, d), lambda bi, c: (layer, gate_row // heads, 0)),
                  pl.BlockSpec((heads, 1), lambda bi, c: (0, 0))],
        out_specs=pl.BlockSpec((1, heads, ts), lambda bi, c: (bi, 0, c)),
        out_shape=jax.ShapeDtypeStruct((batch, heads, seq), F32),
        scratch_shapes=[pltpu.VMEM((heads, 1), F32)],
        compiler_params=_params("parallel", "arbitrary"),
        name="fox_gate",
    )(hn, w_in_t, b.reshape(heads, 1))


def _fox_kernel(q_ref, k_ref, v_ref, cq_ref, ck_ref, o_ref, cs_ref, m_ref, l_ref, acc_ref,
                *, tq, nheads, scale):
    i = pl.program_id(2)
    d = FOX_HEAD_DIM
    nchunk = ck_ref.shape[2]

    @pl.when(i == 0)
    def _():
        r = lax.broadcasted_iota(jnp.int32, (LANE, LANE), 0)
        c = lax.broadcasted_iota(jnp.int32, (LANE, LANE), 1)
        for hh in range(nheads):
            def chunk(t, carry, hh=hh):
                row = ck_ref[0, hh, pl.ds(t, 1), :] * LOG2E
                col = jnp.sum(jnp.where(r == c, row, 0.0), axis=1, keepdims=True)
                cs_ref[hh, pl.ds(pl.multiple_of(t * LANE, LANE), LANE), :] = (
                    jnp.broadcast_to(col, (LANE, LANE)))
                return carry
            lax.fori_loop(0, nchunk, chunk, 0, unroll=8)

    m_ref[...] = jnp.full_like(m_ref, -jnp.inf)
    l_ref[...] = jnp.zeros_like(l_ref)
    acc_ref[...] = jnp.zeros_like(acc_ref)

    def step(j, masked):
        start = pl.multiple_of(j * tq, tq)
        heads = range(nheads)
        cols = [slice(hh * d, (hh + 1) * d) for hh in heads]
        scores = [lax.dot_general(k_ref[pl.ds(start, tq), cols[hh]], q_ref[:, cols[hh]], _NT,
                                  preferred_element_type=F32) for hh in heads]
        ys, shifts, alphas = [], [], []
        for hh in heads:
            ct = cq_ref[0, hh:hh + 1, :] * LOG2E
            cs = jnp.tile(cs_ref[hh, pl.ds(start, tq), :], (1, tq // LANE))
            y = scores[hh] * (scale * LOG2E) - cs
            if masked:
                r = lax.broadcasted_iota(jnp.int32, y.shape, 0)
                c = lax.broadcasted_iota(jnp.int32, y.shape, 1)
                y = jnp.where(c >= r, y, NEG)
            m_prev = m_ref[hh]
            m_new = jnp.maximum(m_prev, jnp.max(y, axis=0, keepdims=True) + ct)
            m_ref[hh] = m_new
            ys.append(y)
            shifts.append(m_new - ct)
            alphas.append(jnp.exp2(m_prev - m_new))
        for hh in heads:
            p = jnp.exp2(ys[hh] - shifts[hh])
            l_ref[hh] = alphas[hh] * l_ref[hh] + jnp.sum(p, axis=0, keepdims=True)
            pv = lax.dot_general(v_ref[pl.ds(start, tq), cols[hh]], p.astype(BF16),
                                 (((0,), (0,)), ((), ())),
                                 preferred_element_type=F32)
            acc_ref[hh] = alphas[hh] * acc_ref[hh] + pv

    def body(j, carry):
        step(j, masked=False)
        return carry

    lax.fori_loop(0, i, body, 0)
    step(i, masked=True)
    for hh in range(nheads):
        out = acc_ref[hh] / l_ref[hh]
        o_ref[:, hh * d:(hh + 1) * d] = out.T.astype(o_ref.dtype)


def _fox_attention(qkv, cum, *, batch, seq, heads, tq=512, nheads=4):
    d = FOX_HEAD_DIM
    assert seq % tq == 0 and tq % LANE == 0 and heads % nheads == 0, (seq, tq, heads)
    assert qkv.shape == (batch * seq, 3 * heads * d), qkv.shape
    nq = seq // tq
    groups = heads // nheads
    w = nheads * d
    cum_q = cum.reshape(batch * groups, nheads, seq)
    cum_k = cum.reshape(batch * groups, nheads, seq // LANE, LANE)
    return pl.pallas_call(
        functools.partial(_fox_kernel, tq=tq, nheads=nheads, scale=d ** -0.5),
        grid=(batch, groups, nq),
        in_specs=[
            pl.BlockSpec((tq, w), lambda b, g, i: (b * nq + i, g)),
            pl.BlockSpec((seq, w), lambda b, g, i: (b, groups + g)),
            pl.BlockSpec((seq, w), lambda b, g, i: (b, 2 * groups + g)),
            pl.BlockSpec((1, nheads, tq), lambda b, g, i: (b * groups + g, 0, i)),
            pl.BlockSpec((1, nheads, seq // LANE, LANE),
                         lambda b, g, i: (b * groups + g, 0, 0, 0)),
        ],
        out_specs=pl.BlockSpec((tq, w), lambda b, g, i: (b * nq + i, g)),
        out_shape=jax.ShapeDtypeStruct((batch * seq, heads * d), BF16),
        scratch_shapes=[pltpu.VMEM((nheads, seq, LANE), F32),
                        pltpu.VMEM((nheads, 1, tq), F32),
                        pltpu.VMEM((nheads, 1, tq), F32),
                        pltpu.VMEM((nheads, d, tq), F32)],
        compiler_params=_params("parallel", "parallel", "arbitrary"),
        name="fox_attention",
    )(qkv, qkv, qkv, cum_q, cum_k)


def _swa_kernel(slope_ref, sink_ref, dist_ref, q_ref, kp_ref, kc_ref, vp_ref, vc_ref, o_ref,
                *, scale):
    pair = pl.program_id(2)
    half = SWA_HEAD_DIM
    npair = SWA_GROUP // 2
    nq = npair * Q_BLOCK

    kband = jnp.concatenate([kp_ref[...], kc_ref[...]], axis=0).astype(F32) * scale
    vband = jnp.concatenate([vp_ref[...], vc_ref[...]], axis=0).astype(F32)
    kroll = pltpu.roll(kband, half, axis=1)
    vroll = pltpu.roll(vband, half, axis=1)
    lo = lax.broadcasted_iota(jnp.int32, kband.shape, 1) < half
    dist = dist_ref[...]
    lane_pair = lax.broadcasted_iota(jnp.int32, (1, nq), 1) // Q_BLOCK

    groups = []
    for cc in range(2):
        k_src, k_alt = (kband, kroll) if cc == 0 else (kroll, kband)
        v_src, v_alt = (vband, vroll) if cc == 0 else (vroll, vband)
        k_sides = (jnp.where(lo, k_src, 0.0).astype(BF16), jnp.where(lo, 0.0, k_alt).astype(BF16))
        v_sides = (jnp.where(lo, v_src, 0.0).astype(BF16), jnp.where(lo, 0.0, v_alt).astype(BF16))
        qs = jnp.concatenate(
            [q_ref[:, (cc * npair + t) * LANE:(cc * npair + t + 1) * LANE]
             for t in range(npair)], axis=0)
        scores = [lax.dot_general(k_sides[par], qs, _NT, preferred_element_type=F32)
                  for par in range(2)]
        groups.append((scores, v_sides))

    for cc, (scores, v_sides) in enumerate(groups):
        out_t = None
        for par in range(2):
            slope = jnp.zeros((1, nq), F32)
            sink = jnp.zeros((1, nq), F32)
            for t in range(npair):
                head = (2 * pair + cc) * SWA_GROUP + 2 * t + par
                slope = jnp.where(lane_pair == t, slope_ref[head], slope)
                sink = jnp.where(lane_pair == t, sink_ref[head], sink)
            s = scores[par] - slope * dist
            m = jnp.maximum(jnp.max(s, axis=0, keepdims=True), sink)
            e = jnp.exp(s - m)
            denom = jnp.sum(e, axis=0, keepdims=True) + jnp.exp(sink - m)
            p = (e * (1.0 / denom)).astype(BF16)
            pv = lax.dot_general(v_sides[par], p, (((0,), (0,)), ((), ())),
                                 preferred_element_type=F32)
            out_t = pv if out_t is None else out_t + pv
        out = out_t.T
        for t in range(npair):
            o_ref[:, (cc * npair + t) * LANE:(cc * npair + t + 1) * LANE] = (
                out[t * Q_BLOCK:(t + 1) * Q_BLOCK].astype(o_ref.dtype))


def _swa_attention(proj, slopes, sinks, *, batch, seq, d_model):
    nb = seq // Q_BLOCK
    kv_heads = d_model // SWA_HEAD_DIM // SWA_GROUP
    npairs = kv_heads // 2
    assert seq % Q_BLOCK == 0 and kv_heads % 2 == 0, (seq, kv_heads)
    assert proj.shape == (batch * seq, d_model + 2 * kv_heads * SWA_HEAD_DIM), proj.shape
    qw = 2 * SWA_GROUP * SWA_HEAD_DIM
    k0 = d_model // LANE
    v0 = k0 + npairs
    prev = lambda b, n, p: b * nb + jnp.maximum(n - 1, 0)
    smem = pl.BlockSpec(memory_space=pltpu.SMEM)
    band = 2 * Q_BLOCK
    nq = (SWA_GROUP // 2) * Q_BLOCK
    return pl.pallas_call(
        functools.partial(_swa_kernel, scale=SWA_HEAD_DIM ** -0.5),
        grid=(batch, nb, npairs),
        in_specs=[
            smem, smem,
            pl.BlockSpec((None, band, nq), lambda b, n, p: (jnp.minimum(n, 1), 0, 0)),
            pl.BlockSpec((Q_BLOCK, qw), lambda b, n, p: (b * nb + n, p)),
            pl.BlockSpec((Q_BLOCK, LANE), lambda b, n, p: (prev(b, n, p), k0 + p)),
            pl.BlockSpec((Q_BLOCK, LANE), lambda b, n, p: (b * nb + n, k0 + p)),
            pl.BlockSpec((Q_BLOCK, LANE), lambda b, n, p: (prev(b, n, p), v0 + p)),
            pl.BlockSpec((Q_BLOCK, LANE), lambda b, n, p: (b * nb + n, v0 + p)),
        ],
        out_specs=pl.BlockSpec((Q_BLOCK, qw), lambda b, n, p: (b * nb + n, p)),
        out_shape=jax.ShapeDtypeStruct((batch * seq, d_model), BF16),
        compiler_params=_params("parallel", "parallel", "arbitrary"),
        name="swa_attention",
    )(slopes, sinks, _swa_distance_table(), proj, proj, proj, proj, proj)


def _swa_distance_table():
    band = 2 * Q_BLOCK
    key = jnp.arange(band)[:, None]
    query = jnp.arange((SWA_GROUP // 2) * Q_BLOCK)[None, :] % Q_BLOCK
    dist = jnp.abs(query - (key - Q_BLOCK)).astype(F32)
    q_chunk = query // CHUNK
    k_chunk = key // CHUNK - WINDOW_CHUNKS
    window = (k_chunk <= q_chunk) & (k_chunk >= q_chunk - WINDOW_CHUNKS)
    first = window & (key >= Q_BLOCK)
    return jnp.stack([jnp.where(first, dist, MASKED_DIST), jnp.where(window, dist, MASKED_DIST)])


def _up_kernel(x_ref, wg_ref, wv_ref, cwg_ref, cwv_ref, cbg_ref, cbv_ref, o_ref,
               ug_ref, uv_ref, tailg_ref, tailv_ref, *, tm, seq_tiles):
    i = pl.program_id(0)
    j = pl.program_id(1)
    x = x_ref[...]
    seq_start = (i % seq_tiles) == 0

    def conv(w_ref, u_ref, tail_ref, cw_ref, cb_ref):
        u_ref[0:SUBLANE, :] = jnp.where(seq_start, 0.0, tail_ref[j])
        u_ref[SUBLANE:SUBLANE + tm, :] = jnp.dot(
            x, w_ref[...].astype(BF16), preferred_element_type=F32)
        tail_ref[j] = u_ref[tm:tm + SUBLANE, :]
        cw = cw_ref[...]
        out = cb_ref[...]
        for tap in range(CONV_WIDTH):
            off = SUBLANE - (CONV_WIDTH - 1) + tap
            out = out + cw[tap:tap + 1, :] * u_ref[off:off + tm, :]
        return out

    gate = conv(wg_ref, ug_ref, tailg_ref, cwg_ref, cbg_ref)
    val = conv(wv_ref, uv_ref, tailv_ref, cwv_ref, cbv_ref)
    o_ref[...] = (gate * (1.0 / (1.0 + jnp.exp(-gate))) * val).astype(o_ref.dtype)


def _conv_ffn_up(hn, w_up, conv_w, conv_b, layer, *, seq, tm=1024, tn=256):
    m, d = hn.shape
    f = w_up.shape[2] // 2
    assert m % tm == 0 and seq % tm == 0 and f % tn == 0, (m, seq, f, tm, tn)
    nj = f // tn
    conv_b = conv_b.reshape(conv_b.shape[0], 1, 2 * f)
    gate = lambda i, j: (layer, 0, j)
    val = lambda i, j: (layer, 0, nj + j)
    slot = pltpu.VMEM((tm + SUBLANE, tn), F32)
    tail = pltpu.VMEM((nj, SUBLANE, tn), F32)
    return pl.pallas_call(
        functools.partial(_up_kernel, tm=tm, seq_tiles=seq // tm),
        grid=(m // tm, nj),
        in_specs=[
            pl.BlockSpec((tm, d), lambda i, j: (i, 0)),
            pl.BlockSpec((None, d, tn), gate),
            pl.BlockSpec((None, d, tn), val),
            pl.BlockSpec((None, CONV_WIDTH, tn), gate),
            pl.BlockSpec((None, CONV_WIDTH, tn), val),
            pl.BlockSpec((None, 1, tn), gate),
            pl.BlockSpec((None, 1, tn), val),
        ],
        out_specs=pl.BlockSpec((tm, tn), lambda i, j: (i, j)),
        out_shape=jax.ShapeDtypeStruct((m, f), BF16),
        scratch_shapes=[slot, slot, tail, tail],
        compiler_params=_params("arbitrary", "arbitrary"),
        name="conv_ffn_up",
    )(hn, w_up, w_up, conv_w, conv_w, conv_b, conv_b)


def kernel(x, attn_norm_g, fox_w_in, fox_b_f, fox_w_o, swa_w_in, swa_sinks, swa_w_o,
           ffn_norm_g, ffn_w_up, ffn_conv_w, ffn_conv_b, ffn_w_down, final_norm_g):
    batch, seq, d_model = x.shape
    depth = attn_norm_g.shape[0]
    fox_heads = fox_b_f.shape[1]
    swa_heads = swa_sinks.shape[1]
    slopes = jnp.exp2(-8.0 * jnp.arange(1, swa_heads + 1, dtype=F32) / swa_heads)

    fox_w_in_t = jnp.swapaxes(fox_w_in, 1, 2)

    h = x.reshape(batch * seq, d_model)
    for layer in range(depth):
        inst = layer // 2
        hn = _rmsnorm(h, attn_norm_g[layer], BF16)
        if layer % 2 == 0:
            qkv = _mm(hn, fox_w_in_t, inst, 3 * d_model, BF16, tm=1024, tn=512,
                      transposed_w=True, name="fox_qkv")
            cum = _fox_gate(hn, fox_w_in_t, inst, 3 * d_model, fox_b_f[inst],
                            batch=batch, seq=seq)
            mix = _fox_attention(qkv, cum, batch=batch, seq=seq, heads=fox_heads)
            w_o = fox_w_o
        else:
            proj = _mm(hn, swa_w_in, inst, swa_w_in.shape[2], BF16,
                       tm=1024, tn=512, name="swa_qkv")
            mix = _swa_attention(proj, slopes, swa_sinks[inst],
                                 batch=batch, seq=seq, d_model=d_model)
            w_o = swa_w_o
        h = _mm(mix, w_o, inst, d_model, F32, tm=1024, tn=512, residual=h, name="attn_out")
        hn = _rmsnorm(h, ffn_norm_g[layer], BF16)
        act = _conv_ffn_up(hn, ffn_w_up, ffn_conv_w, ffn_conv_b, layer, seq=seq)
        h = _mm(act, ffn_w_down, layer, d_model, F32, tm=1024, tn=256,
                residual=h, single_buffer_x=True, name="ffn_down")
    out = _rmsnorm(h, final_norm_g, F32)
    return out.reshape(batch, seq, d_model)
```

```python
import functools

import jax
import jax.numpy as jnp
from jax import lax
from jax.experimental import pallas as pl
from jax.experimental.pallas import tpu as pltpu

LANE = 128
SUBLANE = 8
VMEM_LIMIT_BYTES = 56 * 1024 * 1024

FOX_HEAD_DIM = 128
SWA_HEAD_DIM = 64
SWA_GROUP = 8
Q_BLOCK = 128
CHUNK = 64
WINDOW_CHUNKS = 2
CONV_WIDTH = 3
EPS = 1e-6
NEG = -1e30
MASKED_DIST = 1e30
LOG2E = 1.4426950408889634

BF16 = jnp.bfloat16
F32 = jnp.float32

_NT = (((1,), (1,)), ((), ()))


def _params(*semantics):
    return pltpu.CompilerParams(
        dimension_semantics=semantics, vmem_limit_bytes=VMEM_LIMIT_BYTES)


def _rmsnorm_kernel(x_ref, g_ref, o_ref):
    x = x_ref[...]
    ms = jnp.mean(x * x, axis=-1, keepdims=True)
    o_ref[...] = (x * lax.rsqrt(ms + EPS) * g_ref[...]).astype(o_ref.dtype)


def _rmsnorm(x, g, out_dtype, *, rows=256):
    m, d = x.shape
    assert m % rows == 0 and d % LANE == 0, (x.shape, rows)
    return pl.pallas_call(
        _rmsnorm_kernel,
        grid=(m // rows,),
        in_specs=[pl.BlockSpec((rows, d), lambda i: (i, 0)),
                  pl.BlockSpec((1, d), lambda i: (0, 0))],
        out_specs=pl.BlockSpec((rows, d), lambda i: (i, 0)),
        out_shape=jax.ShapeDtypeStruct((m, d), out_dtype),
        compiler_params=_params("parallel"),
        name="rmsnorm",
    )(x, g.reshape(1, d))


def _mm_kernel(*refs, transposed_w, has_residual):
    x_ref, w_ref = refs[:2]
    o_ref = refs[-1]
    w = w_ref[...].astype(BF16)
    if transposed_w:
        acc = lax.dot_general(x_ref[...], w, _NT, preferred_element_type=F32)
    else:
        acc = jnp.dot(x_ref[...], w, preferred_element_type=F32)
    if has_residual:
        acc = refs[2][...] + acc
    o_ref[...] = acc.astype(o_ref.dtype)


def _mm(x, w, layer, n, out_dtype, *, tm, tn, residual=None, transposed_w=False,
        single_buffer_x=False, name="mm"):
    m, k = x.shape
    k_axis, n_axis = (2, 1) if transposed_w else (1, 2)
    assert w.shape[k_axis] == k and w.shape[n_axis] >= n, (x.shape, w.shape, n)
    assert m % tm == 0 and n % tn == 0, (m, n, tm, tn)
    x_mode = dict(pipeline_mode=pl.Buffered(1)) if single_buffer_x else {}
    if transposed_w:
        w_spec = pl.BlockSpec((None, tn, k), lambda i, j: (layer, j, 0))
    else:
        w_spec = pl.BlockSpec((None, k, tn), lambda i, j: (layer, 0, j))
    in_specs = [pl.BlockSpec((tm, k), lambda i, j: (i, 0), **x_mode), w_spec]
    args = [x, w]
    if residual is not None:
        in_specs.append(pl.BlockSpec((tm, tn), lambda i, j: (i, j)))
        args.append(residual)
    return pl.pallas_call(
        functools.partial(_mm_kernel, transposed_w=transposed_w,
                          has_residual=residual is not None),
        grid=(m // tm, n // tn),
        in_specs=in_specs,
        out_specs=pl.BlockSpec((tm, tn), lambda i, j: (i, j)),
        out_shape=jax.ShapeDtypeStruct((m, n), out_dtype),
        compiler_params=_params("parallel", "arbitrary"),
        name=name,
    )(*args)


def _gate_kernel(x_ref, wt_ref, b_ref, o_ref, carry_ref, *, ts):
    c = pl.program_id(1)

    @pl.when(c == 0)
    def _():
        carry_ref[...] = jnp.zeros_like(carry_ref)

    f = lax.dot_general(wt_ref[...].astype(BF16), x_ref[...], _NT,
                        preferred_element_type=F32) + b_ref[...]
    s = jnp.minimum(f, 0.0) - jnp.log1p(jnp.exp(-jnp.abs(f)))
    lane = lax.broadcasted_iota(jnp.int32, s.shape, 1)
    shift = 1
    while shift < ts:
        s = s + jnp.where(lane >= shift, pltpu.roll(s, shift, axis=1), 0.0)
        shift *= 2
    s = s + carry_ref[...]
    o_ref[0] = s
    carry_ref[...] = s[:, ts - 1:ts]


def _fox_gate(hn, w_in_t, layer, gate_row, b, *, batch, seq, ts=1024):
    heads = b.shape[0]
    d = hn.shape[1]
    assert seq % ts == 0 and heads % SUBLANE == 0, (seq, ts, heads)
    assert gate_row % heads == 0 and gate_row + heads <= w_in_t.shape[1], (gate_row, w_in_t.shape)
    nchunk = seq // ts
    return pl.pallas_call(
        functools.partial(_gate_kernel, ts=ts),
        grid=(batch, nchunk),
        in_specs=[pl.BlockSpec((ts, d), lambda bi, c: (bi * nchunk + c, 0)),
                  pl.BlockSpec((None, heads, d), lambda bi, c: (layer, gate_row // heads, 0)),
                  pl.BlockSpec((heads, 1), lambda bi, c: (0, 0))],
        out_specs=pl.BlockSpec((1, heads, ts), lambda bi, c: (bi, 0, c)),
        out_shape=jax.ShapeDtypeStruct((batch, heads, seq), F32),
        scratch_shapes=[pltpu.VMEM((heads, 1), F32)],
        compiler_params=_params("parallel", "arbitrary"),
        name="fox_gate",
    )(hn, w_in_t, b.reshape(heads, 1))


def _fox_kernel(q_ref, k_ref, v_ref, cq_ref, ck_ref, o_ref, cs_ref, m_ref, l_ref, acc_ref,
                *, tq, nheads, scale):
    i = pl.program_id(2)
    d = FOX_HEAD_DIM
    nchunk = ck_ref.shape[2]

    @pl.when(i == 0)
    def _():
        r = lax.broadcasted_iota(jnp.int32, (LANE, LANE), 0)
        c = lax.broadcasted_iota(jnp.int32, (LANE, LANE), 1)
        for hh in range(nheads):
            def chunk(t, carry, hh=hh):
                row = ck_ref[0, hh, pl.ds(t, 1), :] * LOG2E
                col = jnp.sum(jnp.where(r == c, row, 0.0), axis=1, keepdims=True)
                cs_ref[hh, pl.ds(pl.multiple_of(t * LANE, LANE), LANE), :] = (
                    jnp.broadcast_to(col, (LANE, LANE)))
                return carry
            lax.fori_loop(0, nchunk, chunk, 0, unroll=8)

    m_ref[...] = jnp.full_like(m_ref, -jnp.inf)
    l_ref[...] = jnp.zeros_like(l_ref)
    acc_ref[...] = jnp.zeros_like(acc_ref)

    def step(j, masked):
        start = pl.multiple_of(j * tq, tq)
        heads = range(nheads)
        cols = [slice(hh * d, (hh + 1) * d) for hh in heads]
        scores = [lax.dot_general(k_ref[pl.ds(start, tq), cols[hh]], q_ref[:, cols[hh]], _NT,
                                  preferred_element_type=F32) for hh in heads]
        ys, shifts, alphas = [], [], []
        for hh in heads:
            ct = cq_ref[0, hh:hh + 1, :] * LOG2E
            cs = jnp.tile(cs_ref[hh, pl.ds(start, tq), :], (1, tq // LANE))
            y = scores[hh] * (scale * LOG2E) - cs
            if masked:
                r = lax.broadcasted_iota(jnp.int32, y.shape, 0)
                c = lax.broadcasted_iota(jnp.int32, y.shape, 1)
                y = jnp.where(c >= r, y, NEG)
            m_prev = m_ref[hh]
            m_new = jnp.maximum(m_prev, jnp.max(y, axis=0, keepdims=True) + ct)
            m_ref[hh] = m_new
            ys.append(y)
            shifts.append(m_new - ct)
            alphas.append(jnp.exp2(m_prev - m_new))
        for hh in heads:
            p = jnp.exp2(ys[hh] - shifts[hh])
            l_ref[hh] = alphas[hh] * l_ref[hh] + jnp.sum(p, axis=0, keepdims=True)
            pv = lax.dot_general(v_ref[pl.ds(start, tq), cols[hh]], p.astype(BF16),
                                 (((0,), (0,)), ((), ())),
                                 preferred_element_type=F32)
            acc_ref[hh] = alphas[hh] * acc_ref[hh] + pv

    def body(j, carry):
        step(j, masked=False)
        return carry

    lax.fori_loop(0, i, body, 0)
    step(i, masked=True)
    for hh in range(nheads):
        out = acc_ref[hh] / l_ref[hh]
        o_ref[:, hh * d:(hh + 1) * d] = out.T.astype(o_ref.dtype)


def _fox_attention(qkv, cum, *, batch, seq, heads, tq=512, nheads=4):
    d = FOX_HEAD_DIM
    assert seq % tq == 0 and tq % LANE == 0 and heads % nheads == 0, (seq, tq, heads)
    assert qkv.shape == (batch * seq, 3 * heads * d), qkv.shape
    nq = seq // tq
    groups = heads // nheads
    w = nheads * d
    cum_q = cum.reshape(batch * groups, nheads, seq)
    cum_k = cum.reshape(batch * groups, nheads, seq // LANE, LANE)
    return pl.pallas_call(
        functools.partial(_fox_kernel, tq=tq, nheads=nheads, scale=d ** -0.5),
        grid=(batch, groups, nq),
        in_specs=[
            pl.BlockSpec((tq, w), lambda b, g, i: (b * nq + i, g)),
            pl.BlockSpec((seq, w), lambda b, g, i: (b, groups + g)),
            pl.BlockSpec((seq, w), lambda b, g, i: (b, 2 * groups + g)),
            pl.BlockSpec((1, nheads, tq), lambda b, g, i: (b * groups + g, 0, i)),
            pl.BlockSpec((1, nheads, seq // LANE, LANE),
                         lambda b, g, i: (b * groups + g, 0, 0, 0)),
        ],
        out_specs=pl.BlockSpec((tq, w), lambda b, g, i: (b * nq + i, g)),
        out_shape=jax.ShapeDtypeStruct((batch * seq, heads * d), BF16),
        scratch_shapes=[pltpu.VMEM((nheads, seq, LANE), F32),
                        pltpu.VMEM((nheads, 1, tq), F32),
                        pltpu.VMEM((nheads, 1, tq), F32),
                        pltpu.VMEM((nheads, d, tq), F32)],
        compiler_params=_params("parallel", "parallel", "arbitrary"),
        name="fox_attention",
    )(qkv, qkv, qkv, cum_q, cum_k)


def _swa_kernel(slope_ref, sink_ref, dist_ref, q_ref, kp_ref, kc_ref, vp_ref, vc_ref, o_ref,
                *, scale):
    pair = pl.program_id(2)
    half = SWA_HEAD_DIM
    npair = SWA_GROUP // 2
    nq = npair * Q_BLOCK

    kband = jnp.concatenate([kp_ref[...], kc_ref[...]], axis=0).astype(F32) * scale
    vband = jnp.concatenate([vp_ref[...], vc_ref[...]], axis=0).astype(F32)
    kroll = pltpu.roll(kband, half, axis=1)
    vroll = pltpu.roll(vband, half, axis=1)
    lo = lax.broadcasted_iota(jnp.int32, kband.shape, 1) < half
    dist = dist_ref[...]
    lane_pair = lax.broadcasted_iota(jnp.int32, (1, nq), 1) // Q_BLOCK

    groups = []
    for cc in range(2):
        k_src, k_alt = (kband, kroll) if cc == 0 else (kroll, kband)
        v_src, v_alt = (vband, vroll) if cc == 0 else (vroll, vband)
        k_sides = (jnp.where(lo, k_src, 0.0).astype(BF16), jnp.where(lo, 0.0, k_alt).astype(BF16))
        v_sides = (jnp.where(lo, v_src, 0.0).astype(BF16), jnp.where(lo, 0.0, v_alt).astype(BF16))
        qs = jnp.concatenate(
            [q_ref[:, (cc * npair + t) * LANE:(cc * npair + t + 1) * LANE]
             for t in range(npair)], axis=0)
        scores = [lax.dot_general(k_sides[par], qs, _NT, preferred_element_type=F32)
                  for par in range(2)]
        groups.append((scores, v_sides))

    for cc, (scores, v_sides) in enumerate(groups):
        out_t = None
        for par in range(2):
            slope = jnp.zeros((1, nq), F32)
            sink = jnp.zeros((1, nq), F32)
            for t in range(npair):
                head = (2 * pair + cc) * SWA_GROUP + 2 * t + par
                slope = jnp.where(lane_pair == t, slope_ref[head], slope)
                sink = jnp.where(lane_pair == t, sink_ref[head], sink)
            s = scores[par] - slope * dist
            m = jnp.maximum(jnp.max(s, axis=0, keepdims=True), sink)
            e = jnp.exp(s - m)
            denom = jnp.sum(e, axis=0, keepdims=True) + jnp.exp(sink - m)
            p = (e * (1.0 / denom)).astype(BF16)
            pv = lax.dot_general(v_sides[par], p, (((0,), (0,)), ((), ())),
                                 preferred_element_type=F32)
            out_t = pv if out_t is None else out_t + pv
        out = out_t.T
        for t in range(npair):
            o_ref[:, (cc * npair + t) * LANE:(cc * npair + t + 1) * LANE] = (
                out[t * Q_BLOCK:(t + 1) * Q_BLOCK].astype(o_ref.dtype))


def _swa_attention(proj, slopes, sinks, *, batch, seq, d_model):
    nb = seq // Q_BLOCK
    kv_heads = d_model // SWA_HEAD_DIM // SWA_GROUP
    npairs = kv_heads // 2
    assert seq % Q_BLOCK == 0 and kv_heads % 2 == 0, (seq, kv_heads)
    assert proj.shape == (batch * seq, d_model + 2 * kv_heads * SWA_HEAD_DIM), proj.shape
    qw = 2 * SWA_GROUP * SWA_HEAD_DIM
    k0 = d_model // LANE
    v0 = k0 + npairs
    prev = lambda b, n, p: b * nb + jnp.maximum(n - 1, 0)
    smem = pl.BlockSpec(memory_space=pltpu.SMEM)
    band = 2 * Q_BLOCK
    nq = (SWA_GROUP // 2) * Q_BLOCK
    return pl.pallas_call(
        functools.partial(_swa_kernel, scale=SWA_HEAD_DIM ** -0.5),
        grid=(batch, nb, npairs),
        in_specs=[
            smem, smem,
            pl.BlockSpec((None, band, nq), lambda b, n, p: (jnp.minimum(n, 1), 0, 0)),
            pl.BlockSpec((Q_BLOCK, qw), lambda b, n, p: (b * nb + n, p)),
            pl.BlockSpec((Q_BLOCK, LANE), lambda b, n, p: (prev(b, n, p), k0 + p)),
            pl.BlockSpec((Q_BLOCK, LANE), lambda b, n, p: (b * nb + n, k0 + p)),
            pl.BlockSpec((Q_BLOCK, LANE), lambda b, n, p: (prev(b, n, p), v0 + p)),
            pl.BlockSpec((Q_BLOCK, LANE), lambda b, n, p: (b * nb + n, v0 + p)),
        ],
        out_specs=pl.BlockSpec((Q_BLOCK, qw), lambda b, n, p: (b * nb + n, p)),
        out_shape=jax.ShapeDtypeStruct((batch * seq, d_model), BF16),
        compiler_params=_params("parallel", "parallel", "arbitrary"),
        name="swa_attention",
    )(slopes, sinks, _swa_distance_table(), proj, proj, proj, proj, proj)


def _swa_distance_table():
    band = 2 * Q_BLOCK
    key = jnp.arange(band)[:, None]
    query = jnp.arange((SWA_GROUP // 2) * Q_BLOCK)[None, :] % Q_BLOCK
    dist = jnp.abs(query - (key - Q_BLOCK)).astype(F32)
    q_chunk = query // CHUNK
    k_chunk = key // CHUNK - WINDOW_CHUNKS
    window = (k_chunk <= q_chunk) & (k_chunk >= q_chunk - WINDOW_CHUNKS)
    first = window & (key >= Q_BLOCK)
    return jnp.stack([jnp.where(first, dist, MASKED_DIST), jnp.where(window, dist, MASKED_DIST)])


def _up_kernel(x_ref, wg_ref, wv_ref, cwg_ref, cwv_ref, cbg_ref, cbv_ref, o_ref,
               ug_ref, uv_ref, tailg_ref, tailv_ref, *, tm, seq_tiles):
    i = pl.program_id(0)
    j = pl.program_id(1)
    x = x_ref[...]
    seq_start = (i % seq_tiles) == 0

    def conv(w_ref, u_ref, tail_ref, cw_ref, cb_ref):
        u_ref[0:SUBLANE, :] = jnp.where(seq_start, 0.0, tail_ref[j])
        u_ref[SUBLANE:SUBLANE + tm, :] = jnp.dot(
            x, w_ref[...].astype(BF16), preferred_element_type=F32)
        tail_ref[j] = u_ref[tm:tm + SUBLANE, :]
        cw = cw_ref[...]
        out = cb_ref[...]
        for tap in range(CONV_WIDTH):
            off = SUBLANE - (CONV_WIDTH - 1) + tap
            out = out + cw[tap:tap + 1, :] * u_ref[off:off + tm, :]
        return out

    gate = conv(wg_ref, ug_ref, tailg_ref, cwg_ref, cbg_ref)
    val = conv(wv_ref, uv_ref, tailv_ref, cwv_ref, cbv_ref)
    o_ref[...] = (gate * (1.0 / (1.0 + jnp.exp(-gate))) * val).astype(o_ref.dtype)


def _conv_ffn_up(hn, w_up, conv_w, conv_b, layer, *, seq, tm=2048, tn=256):
    m, d = hn.shape
    f = w_up.shape[2] // 2
    assert m % tm == 0 and seq % tm == 0 and f % tn == 0, (m, seq, f, tm, tn)
    nj = f // tn
    conv_b = conv_b.reshape(conv_b.shape[0], 1, 2 * f)
    gate = lambda i, j: (layer, 0, j)
    val = lambda i, j: (layer, 0, nj + j)
    slot = pltpu.VMEM((tm + SUBLANE, tn), F32)
    tail = pltpu.VMEM((nj, SUBLANE, tn), F32)
    return pl.pallas_call(
        functools.partial(_up_kernel, tm=tm, seq_tiles=seq // tm),
        grid=(m // tm, nj),
        in_specs=[
            pl.BlockSpec((tm, d), lambda i, j: (i, 0), pipeline_mode=pl.Buffered(1)),
            pl.BlockSpec((None, d, tn), gate),
            pl.BlockSpec((None, d, tn), val),
            pl.BlockSpec((None, CONV_WIDTH, tn), gate),
            pl.BlockSpec((None, CONV_WIDTH, tn), val),
            pl.BlockSpec((None, 1, tn), gate),
            pl.BlockSpec((None, 1, tn), val),
        ],
        out_specs=pl.BlockSpec((tm, tn), lambda i, j: (i, j)),
        out_shape=jax.ShapeDtypeStruct((m, f), BF16),
        scratch_shapes=[slot, slot, tail, tail],
        compiler_params=_params("arbitrary", "arbitrary"),
        name="conv_ffn_up",
    )(hn, w_up, w_up, conv_w, conv_w, conv_b, conv_b)


def kernel(x, attn_norm_g, fox_w_in, fox_b_f, fox_w_o, swa_w_in, swa_sinks, swa_w_o,
           ffn_norm_g, ffn_w_up, ffn_conv_w, ffn_conv_b, ffn_w_down, final_norm_g):
    batch, seq, d_model = x.shape
    depth = attn_norm_g.shape[0]
    fox_heads = fox_b_f.shape[1]
    swa_heads = swa_sinks.shape[1]
    slopes = jnp.exp2(-8.0 * jnp.arange(1, swa_heads + 1, dtype=F32) / swa_heads)

    fox_w_in_t = jnp.swapaxes(fox_w_in, 1, 2)

    h = x.reshape(batch * seq, d_model)
    for layer in range(depth):
        inst = layer // 2
        hn = _rmsnorm(h, attn_norm_g[layer], BF16)
        if layer % 2 == 0:
            qkv = _mm(hn, fox_w_in_t, inst, 3 * d_model, BF16, tm=2048, tn=512,
                      transposed_w=True, single_buffer_x=True, name="fox_qkv")
            cum = _fox_gate(hn, fox_w_in_t, inst, 3 * d_model, fox_b_f[inst],
                            batch=batch, seq=seq)
            mix = _fox_attention(qkv, cum, batch=batch, seq=seq, heads=fox_heads)
            w_o = fox_w_o
        else:
            proj = _mm(hn, swa_w_in, inst, swa_w_in.shape[2], BF16,
                       tm=2048, tn=512, single_buffer_x=True, name="swa_qkv")
            mix = _swa_attention(proj, slopes, swa_sinks[inst],
                                 batch=batch, seq=seq, d_model=d_model)
            w_o = swa_w_o
        h = _mm(mix, w_o, inst, d_model, F32, tm=2048, tn=512, residual=h,
                single_buffer_x=True, name="attn_out")
        hn = _rmsnorm(h, ffn_norm_g[layer], BF16)
        act = _conv_ffn_up(hn, ffn_w_up, ffn_conv_w, ffn_conv_b, layer, seq=seq)
        h = _mm(act, ffn_w_down, layer, d_model, F32, tm=1024, tn=256,
                residual=h, single_buffer_x=True, name="ffn_down")
    out = _rmsnorm(h, final_norm_g, F32)
    return out.reshape(batch, seq, d_model)
```

```python
import functools

import jax
import jax.numpy as jnp
from jax import lax
from jax.experimental import pallas as pl
from jax.experimental.pallas import tpu as pltpu

LANE = 128
SUBLANE = 8
VMEM_LIMIT_BYTES = 56 * 1024 * 1024

FOX_HEAD_DIM = 128
SWA_HEAD_DIM = 64
SWA_GROUP = 8
Q_BLOCK = 128
CHUNK = 64
WINDOW_CHUNKS = 2
CONV_WIDTH = 3
EPS = 1e-6
NEG = -1e30
MASKED_DIST = 1e30
LOG2E = 1.4426950408889634

BF16 = jnp.bfloat16
F32 = jnp.float32

_NT = (((1,), (1,)), ((), ()))


def _params(*semantics):
    return pltpu.CompilerParams(
        dimension_semantics=semantics, vmem_limit_bytes=VMEM_LIMIT_BYTES)


def _rmsnorm_kernel(x_ref, g_ref, o_ref):
    x = x_ref[...]
    ms = jnp.mean(x * x, axis=-1, keepdims=True)
    o_ref[...] = (x * lax.rsqrt(ms + EPS) * g_ref[...]).astype(o_ref.dtype)


def _rmsnorm(x, g, out_dtype, *, rows=512):
    m, d = x.shape
    assert m % rows == 0 and d % LANE == 0, (x.shape, rows)
    return pl.pallas_call(
        _rmsnorm_kernel,
        grid=(m // rows,),
        in_specs=[pl.BlockSpec((rows, d), lambda i: (i, 0)),
                  pl.BlockSpec((1, d), lambda i: (0, 0))],
        out_specs=pl.BlockSpec((rows, d), lambda i: (i, 0)),
        out_shape=jax.ShapeDtypeStruct((m, d), out_dtype),
        compiler_params=_params("parallel"),
        name="rmsnorm",
    )(x, g.reshape(1, d))


def _mm_kernel(*refs, transposed_w, has_residual):
    x_ref, w_ref = refs[:2]
    o_ref = refs[-1]
    w = w_ref[...].astype(BF16)
    if transposed_w:
        acc = lax.dot_general(x_ref[...], w, _NT, preferred_element_type=F32)
    else:
        acc = jnp.dot(x_ref[...], w, preferred_element_type=F32)
    if has_residual:
        acc = refs[2][...] + acc
    o_ref[...] = acc.astype(o_ref.dtype)


def _mm(x, w, layer, n, out_dtype, *, tm, tn, residual=None, transposed_w=False,
        single_buffer_x=False, name="mm"):
    m, k = x.shape
    k_axis, n_axis = (2, 1) if transposed_w else (1, 2)
    assert w.shape[k_axis] == k and w.shape[n_axis] >= n, (x.shape, w.shape, n)
    assert m % tm == 0 and n % tn == 0, (m, n, tm, tn)
    x_mode = dict(pipeline_mode=pl.Buffered(1)) if single_buffer_x else {}
    if transposed_w:
        w_spec = pl.BlockSpec((None, tn, k), lambda i, j: (layer, j, 0))
    else:
        w_spec = pl.BlockSpec((None, k, tn), lambda i, j: (layer, 0, j))
    in_specs = [pl.BlockSpec((tm, k), lambda i, j: (i, 0), **x_mode), w_spec]
    args = [x, w]
    if residual is not None:
        in_specs.append(pl.BlockSpec((tm, tn), lambda i, j: (i, j)))
        args.append(residual)
    return pl.pallas_call(
        functools.partial(_mm_kernel, transposed_w=transposed_w,
                          has_residual=residual is not None),
        grid=(m // tm, n // tn),
        in_specs=in_specs,
        out_specs=pl.BlockSpec((tm, tn), lambda i, j: (i, j)),
        out_shape=jax.ShapeDtypeStruct((m, n), out_dtype),
        compiler_params=_params("parallel", "arbitrary"),
        name=name,
    )(*args)


def _gate_kernel(x_ref, wt_ref, b_ref, o_ref, carry_ref, *, ts):
    c = pl.program_id(1)

    @pl.when(c == 0)
    def _():
        carry_ref[...] = jnp.zeros_like(carry_ref)

    f = lax.dot_general(wt_ref[...].astype(BF16), x_ref[...], _NT,
                        preferred_element_type=F32) + b_ref[...]
    s = jnp.minimum(f, 0.0) - jnp.log1p(jnp.exp(-jnp.abs(f)))
    lane = lax.broadcasted_iota(jnp.int32, s.shape, 1)
    shift = 1
    while shift < ts:
        s = s + jnp.where(lane >= shift, pltpu.roll(s, shift, axis=1), 0.0)
        shift *= 2
    s = s + carry_ref[...]
    o_ref[0] = s
    carry_ref[...] = s[:, ts - 1:ts]


def _fox_gate(hn, w_in_t, layer, gate_row, b, *, batch, seq, ts=1024):
    heads = b.shape[0]
    d = hn.shape[1]
    assert seq % ts == 0 and heads % SUBLANE == 0, (seq, ts, heads)
    assert gate_row % heads == 0 and gate_row + heads <= w_in_t.shape[1], (gate_row, w_in_t.shape)
    nchunk = seq // ts
    return pl.pallas_call(
        functools.partial(_gate_kernel, ts=ts),
        grid=(batch, nchunk),
        in_specs=[pl.BlockSpec((ts, d), lambda bi, c: (bi * nchunk + c, 0)),
                  pl.BlockSpec((None, heads, d), lambda bi, c: (layer, gate_row // heads, 0)),
                  pl.BlockSpec((heads, 1), lambda bi, c: (0, 0))],
        out_specs=pl.BlockSpec((1, heads, ts), lambda bi, c: (bi, 0, c)),
        out_shape=jax.ShapeDtypeStruct((batch, heads, seq), F32),
        scratch_shapes=[pltpu.VMEM((heads, 1), F32)],
        compiler_params=_params("parallel", "arbitrary"),
        name="fox_gate",
    )(hn, w_in_t, b.reshape(heads, 1))


def _fox_kernel(q_ref, k_ref, v_ref, cq_ref, ck_ref, o_ref, cs_ref, m_ref, l_ref, acc_ref,
                *, tq, nheads, scale):
    i = pl.program_id(2)
    d = FOX_HEAD_DIM
    nchunk = ck_ref.shape[2]

    @pl.when(i == 0)
    def _():
        r = lax.broadcasted_iota(jnp.int32, (LANE, LANE), 0)
        c = lax.broadcasted_iota(jnp.int32, (LANE, LANE), 1)
        for hh in range(nheads):
            def chunk(t, carry, hh=hh):
                row = ck_ref[0, hh, pl.ds(t, 1), :] * LOG2E
                col = jnp.sum(jnp.where(r == c, row, 0.0), axis=1, keepdims=True)
                cs_ref[hh, pl.ds(pl.multiple_of(t * LANE, LANE), LANE), :] = (
                    jnp.broadcast_to(col, (LANE, LANE)))
                return carry
            lax.fori_loop(0, nchunk, chunk, 0, unroll=8)

    m_ref[...] = jnp.full_like(m_ref, -jnp.inf)
    l_ref[...] = jnp.zeros_like(l_ref)
    acc_ref[...] = jnp.zeros_like(acc_ref)

    def step(j, masked):
        start = pl.multiple_of(j * tq, tq)
        heads = range(nheads)
        cols = [slice(hh * d, (hh + 1) * d) for hh in heads]
        scores = [lax.dot_general(k_ref[pl.ds(start, tq), cols[hh]], q_ref[:, cols[hh]], _NT,
                                  preferred_element_type=F32) for hh in heads]
        ys, shifts, alphas = [], [], []
        for hh in heads:
            ct = cq_ref[0, hh:hh + 1, :] * LOG2E
            cs = jnp.tile(cs_ref[hh, pl.ds(start, tq), :], (1, tq // LANE))
            y = scores[hh] * (scale * LOG2E) - cs
            if masked:
                r = lax.broadcasted_iota(jnp.int32, y.shape, 0)
                c = lax.broadcasted_iota(jnp.int32, y.shape, 1)
                y = jnp.where(c >= r, y, NEG)
            m_prev = m_ref[hh]
            m_new = jnp.maximum(m_prev, jnp.max(y, axis=0, keepdims=True) + ct)
            m_ref[hh] = m_new
            ys.append(y)
            shifts.append(m_new - ct)
            alphas.append(jnp.exp2(m_prev - m_new))
        for hh in heads:
            p = jnp.exp2(ys[hh] - shifts[hh])
            l_ref[hh] = alphas[hh] * l_ref[hh] + jnp.sum(p, axis=0, keepdims=True)
            pv = lax.dot_general(v_ref[pl.ds(start, tq), cols[hh]], p.astype(BF16),
                                 (((0,), (0,)), ((), ())),
                                 preferred_element_type=F32)
            acc_ref[hh] = alphas[hh] * acc_ref[hh] + pv

    def body(j, carry):
        step(j, masked=False)
        return carry

    lax.fori_loop(0, i, body, 0)
    step(i, masked=True)
    for hh in range(nheads):
        out = acc_ref[hh] / l_ref[hh]
        o_ref[:, hh * d:(hh + 1) * d] = out.T.astype(o_ref.dtype)


def _fox_attention(qkv, cum, *, batch, seq, heads, tq=512, nheads=4):
    d = FOX_HEAD_DIM
    assert seq % tq == 0 and tq % LANE == 0 and heads % nheads == 0, (seq, tq, heads)
    assert qkv.shape == (batch * seq, 3 * heads * d), qkv.shape
    nq = seq // tq
    groups = heads // nheads
    w = nheads * d
    cum_q = cum.reshape(batch * groups, nheads, seq)
    cum_k = cum.reshape(batch * groups, nheads, seq // LANE, LANE)
    return pl.pallas_call(
        functools.partial(_fox_kernel, tq=tq, nheads=nheads, scale=d ** -0.5),
        grid=(batch, groups, nq),
        in_specs=[
            pl.BlockSpec((tq, w), lambda b, g, i: (b * nq + i, g)),
            pl.BlockSpec((seq, w), lambda b, g, i: (b, groups + g)),
            pl.BlockSpec((seq, w), lambda b, g, i: (b, 2 * groups + g)),
            pl.BlockSpec((1, nheads, tq), lambda b, g, i: (b * groups + g, 0, i)),
            pl.BlockSpec((1, nheads, seq // LANE, LANE),
                         lambda b, g, i: (b * groups + g, 0, 0, 0)),
        ],
        out_specs=pl.BlockSpec((tq, w), lambda b, g, i: (b * nq + i, g)),
        out_shape=jax.ShapeDtypeStruct((batch * seq, heads * d), BF16),
        scratch_shapes=[pltpu.VMEM((nheads, seq, LANE), F32),
                        pltpu.VMEM((nheads, 1, tq), F32),
                        pltpu.VMEM((nheads, 1, tq), F32),
                        pltpu.VMEM((nheads, d, tq), F32)],
        compiler_params=_params("parallel", "parallel", "arbitrary"),
        name="fox_attention",
    )(qkv, qkv, qkv, cum_q, cum_k)


def _swa_kernel(slope_ref, sink_ref, dist_ref, q_ref, kp_ref, kc_ref, vp_ref, vc_ref, o_ref,
                *, scale, pairs):
    half = SWA_HEAD_DIM
    npair = SWA_GROUP // 2
    nq = npair * Q_BLOCK
    lo = lax.broadcasted_iota(jnp.int32, (2 * Q_BLOCK, LANE), 1) < half
    dist = dist_ref[...]
    lane_pair = lax.broadcasted_iota(jnp.int32, (1, nq), 1) // Q_BLOCK

    groups = []
    for pp in range(pairs):
        lanes = slice(pp * LANE, (pp + 1) * LANE)
        kband = jnp.concatenate([kp_ref[:, lanes], kc_ref[:, lanes]], axis=0).astype(F32) * scale
        vband = jnp.concatenate([vp_ref[:, lanes], vc_ref[:, lanes]], axis=0).astype(F32)
        kroll = pltpu.roll(kband, half, axis=1)
        vroll = pltpu.roll(vband, half, axis=1)
        for cc in range(2):
            k_src, k_alt = (kband, kroll) if cc == 0 else (kroll, kband)
            v_src, v_alt = (vband, vroll) if cc == 0 else (vroll, vband)
            k_sides = (jnp.where(lo, k_src, 0.0).astype(BF16),
                       jnp.where(lo, 0.0, k_alt).astype(BF16))
            v_sides = (jnp.where(lo, v_src, 0.0).astype(BF16),
                       jnp.where(lo, 0.0, v_alt).astype(BF16))
            kv_head = 2 * pp + cc
            qs = jnp.concatenate(
                [q_ref[:, (kv_head * npair + t) * LANE:(kv_head * npair + t + 1) * LANE]
                 for t in range(npair)], axis=0)
            scores = [lax.dot_general(k_sides[par], qs, _NT, preferred_element_type=F32)
                      for par in range(2)]
            groups.append((kv_head, scores, v_sides))

    first_head = pl.program_id(2) * (2 * pairs * SWA_GROUP)
    for kv_head, scores, v_sides in groups:
        out_t = None
        for par in range(2):
            slope = jnp.zeros((1, nq), F32)
            sink = jnp.zeros((1, nq), F32)
            for t in range(npair):
                head = first_head + kv_head * SWA_GROUP + 2 * t + par
                slope = jnp.where(lane_pair == t, slope_ref[head], slope)
                sink = jnp.where(lane_pair == t, sink_ref[head], sink)
            s = scores[par] - slope * dist
            m = jnp.maximum(jnp.max(s, axis=0, keepdims=True), sink)
            e = jnp.exp(s - m)
            denom = jnp.sum(e, axis=0, keepdims=True) + jnp.exp(sink - m)
            p = (e * (1.0 / denom)).astype(BF16)
            pv = lax.dot_general(v_sides[par], p, (((0,), (0,)), ((), ())),
                                 preferred_element_type=F32)
            out_t = pv if out_t is None else out_t + pv
        out = out_t.T
        for t in range(npair):
            o_ref[:, (kv_head * npair + t) * LANE:(kv_head * npair + t + 1) * LANE] = (
                out[t * Q_BLOCK:(t + 1) * Q_BLOCK].astype(o_ref.dtype))


def _swa_attention(proj, slopes, sinks, *, batch, seq, d_model, max_pairs=4):
    nb = seq // Q_BLOCK
    kv_heads = d_model // SWA_HEAD_DIM // SWA_GROUP
    pairs = min(max_pairs, kv_heads // 2)
    assert seq % Q_BLOCK == 0 and kv_heads % (2 * pairs) == 0, (seq, kv_heads, pairs)
    assert proj.shape == (batch * seq, d_model + 2 * kv_heads * SWA_HEAD_DIM), proj.shape
    steps = kv_heads // (2 * pairs)
    qw = 2 * pairs * SWA_GROUP * SWA_HEAD_DIM
    kw = pairs * LANE
    k0 = d_model // kw
    v0 = k0 + steps
    prev = lambda b, n, p: b * nb + jnp.maximum(n - 1, 0)
    smem = pl.BlockSpec(memory_space=pltpu.SMEM)
    band = 2 * Q_BLOCK
    nq = (SWA_GROUP // 2) * Q_BLOCK
    return pl.pallas_call(
        functools.partial(_swa_kernel, scale=SWA_HEAD_DIM ** -0.5, pairs=pairs),
        grid=(batch, nb, steps),
        in_specs=[
            smem, smem,
            pl.BlockSpec((None, band, nq), lambda b, n, p: (jnp.minimum(n, 1), 0, 0)),
            pl.BlockSpec((Q_BLOCK, qw), lambda b, n, p: (b * nb + n, p)),
            pl.BlockSpec((Q_BLOCK, kw), lambda b, n, p: (prev(b, n, p), k0 + p)),
            pl.BlockSpec((Q_BLOCK, kw), lambda b, n, p: (b * nb + n, k0 + p)),
            pl.BlockSpec((Q_BLOCK, kw), lambda b, n, p: (prev(b, n, p), v0 + p)),
            pl.BlockSpec((Q_BLOCK, kw), lambda b, n, p: (b * nb + n, v0 + p)),
        ],
        out_specs=pl.BlockSpec((Q_BLOCK, qw), lambda b, n, p: (b * nb + n, p)),
        out_shape=jax.ShapeDtypeStruct((batch * seq, d_model), BF16),
        compiler_params=_params("parallel", "parallel", "arbitrary"),
        name="swa_attention",
    )(slopes, sinks, _swa_distance_table(), proj, proj, proj, proj, proj)


def _swa_distance_table():
    band = 2 * Q_BLOCK
    key = jnp.arange(band)[:, None]
    query = jnp.arange((SWA_GROUP // 2) * Q_BLOCK)[None, :] % Q_BLOCK
    dist = jnp.abs(query - (key - Q_BLOCK)).astype(F32)
    q_chunk = query // CHUNK
    k_chunk = key // CHUNK - WINDOW_CHUNKS
    window = (k_chunk <= q_chunk) & (k_chunk >= q_chunk - WINDOW_CHUNKS)
    first = window & (key >= Q_BLOCK)
    return jnp.stack([jnp.where(first, dist, MASKED_DIST), jnp.where(window, dist, MASKED_DIST)])


def _up_kernel(x_ref, wg_ref, wv_ref, cwg_ref, cwv_ref, cbg_ref, cbv_ref, o_ref,
               ug_ref, uv_ref, tailg_ref, tailv_ref, *, tm, seq_tiles):
    i = pl.program_id(0)
    j = pl.program_id(1)
    x = x_ref[...]
    seq_start = (i % seq_tiles) == 0

    def conv(w_ref, u_ref, tail_ref, cw_ref, cb_ref):
        u_ref[0:SUBLANE, :] = jnp.where(seq_start, 0.0, tail_ref[j])
        u_ref[SUBLANE:SUBLANE + tm, :] = jnp.dot(
            x, w_ref[...].astype(BF16), preferred_element_type=F32)
        tail_ref[j] = u_ref[tm:tm + SUBLANE, :]
        cw = cw_ref[...]
        out = cb_ref[...]
        for tap in range(CONV_WIDTH):
            off = SUBLANE - (CONV_WIDTH - 1) + tap
            out = out + cw[tap:tap + 1, :] * u_ref[off:off + tm, :]
        return out

    gate = conv(wg_ref, ug_ref, tailg_ref, cwg_ref, cbg_ref)
    val = conv(wv_ref, uv_ref, tailv_ref, cwv_ref, cbv_ref)
    o_ref[...] = (gate * (1.0 / (1.0 + jnp.exp(-gate))) * val).astype(o_ref.dtype)


def _conv_ffn_up(hn, w_up, conv_w, conv_b, layer, *, seq, tm=1024, tn=256):
    m, d = hn.shape
    f = w_up.shape[2] // 2
    assert m % tm == 0 and seq % tm == 0 and f % tn == 0, (m, seq, f, tm, tn)
    nj = f // tn
    conv_b = conv_b.reshape(conv_b.shape[0], 1, 2 * f)
    gate = lambda i, j: (layer, 0, j)
    val = lambda i, j: (layer, 0, nj + j)
    slot = pltpu.VMEM((tm + SUBLANE, tn), F32)
    tail = pltpu.VMEM((nj, SUBLANE, tn), F32)
    return pl.pallas_call(
        functools.partial(_up_kernel, tm=tm, seq_tiles=seq // tm),
        grid=(m // tm, nj),
        in_specs=[
            pl.BlockSpec((tm, d), lambda i, j: (i, 0)),
            pl.BlockSpec((None, d, tn), gate),
            pl.BlockSpec((None, d, tn), val),
            pl.BlockSpec((None, CONV_WIDTH, tn), gate),
            pl.BlockSpec((None, CONV_WIDTH, tn), val),
            pl.BlockSpec((None, 1, tn), gate),
            pl.BlockSpec((None, 1, tn), val),
        ],
        out_specs=pl.BlockSpec((tm, tn), lambda i, j: (i, j)),
        out_shape=jax.ShapeDtypeStruct((m, f), BF16),
        scratch_shapes=[slot, slot, tail, tail],
        compiler_params=_params("arbitrary", "arbitrary"),
        name="conv_ffn_up",
    )(hn, w_up, w_up, conv_w, conv_w, conv_b, conv_b)


def kernel(x, attn_norm_g, fox_w_in, fox_b_f, fox_w_o, swa_w_in, swa_sinks, swa_w_o,
           ffn_norm_g, ffn_w_up, ffn_conv_w, ffn_conv_b, ffn_w_down, final_norm_g):
    batch, seq, d_model = x.shape
    depth = attn_norm_g.shape[0]
    fox_heads = fox_b_f.shape[1]
    swa_heads = swa_sinks.shape[1]
    slopes = jnp.exp2(-8.0 * jnp.arange(1, swa_heads + 1, dtype=F32) / swa_heads)

    fox_w_in_t = jnp.swapaxes(fox_w_in, 1, 2)

    h = x.reshape(batch * seq, d_model)
    for layer in range(depth):
        inst = layer // 2
        hn = _rmsnorm(h, attn_norm_g[layer], BF16)
        if layer % 2 == 0:
            qkv = _mm(hn, fox_w_in_t, inst, 3 * d_model, BF16, tm=2048, tn=512,
                      transposed_w=True, single_buffer_x=True, name="fox_qkv")
            cum = _fox_gate(hn, fox_w_in_t, inst, 3 * d_model, fox_b_f[inst],
                            batch=batch, seq=seq)
            mix = _fox_attention(qkv, cum, batch=batch, seq=seq, heads=fox_heads)
            w_o = fox_w_o
        else:
            proj = _mm(hn, swa_w_in, inst, swa_w_in.shape[2], BF16,
                       tm=2048, tn=512, single_buffer_x=True, name="swa_qkv")
            mix = _swa_attention(proj, slopes, swa_sinks[inst],
                                 batch=batch, seq=seq, d_model=d_model)
            w_o = swa_w_o
        h = _mm(mix, w_o, inst, d_model, F32, tm=1024, tn=512, residual=h, name="attn_out")
        hn = _rmsnorm(h, ffn_norm_g[layer], BF16)
        act = _conv_ffn_up(hn, ffn_w_up, ffn_conv_w, ffn_conv_b, layer, seq=seq)
        h = _mm(act, ffn_w_down, layer, d_model, F32, tm=1024, tn=256,
                residual=h, single_buffer_x=True, name="ffn_down")
    out = _rmsnorm(h, final_norm_g, F32)
    return out.reshape(batch, seq, d_model)
```

```python
import functools

import jax
import jax.numpy as jnp
from jax import lax
from jax.experimental import pallas as pl
from jax.experimental.pallas import tpu as pltpu

LANE = 128
SUBLANE = 8
VMEM_LIMIT_BYTES = 56 * 1024 * 1024

FOX_HEAD_DIM = 128
SWA_HEAD_DIM = 64
SWA_GROUP = 8
Q_BLOCK = 128
CHUNK = 64
WINDOW_CHUNKS = 2
CONV_WIDTH = 3
EPS = 1e-6
NEG = -1e30
MASKED_DIST = 1e30
LOG2E = 1.4426950408889634

BF16 = jnp.bfloat16
F32 = jnp.float32

QKV_TILE = (2048, 512)
ATTN_OUT_TILE = (1024, 512)
FFN_UP_TILE = (1024, 256)
FFN_DOWN_TILE = (1024, 256)
NORM_ROWS = 512
GATE_SEQ_CHUNK = 1024
FOX_Q_TILE = 512
FOX_HEADS_PER_STEP = 4
SWA_MAX_PAIRS_PER_STEP = 4

_NT = (((1,), (1,)), ((), ()))


def _params(*semantics):
    return pltpu.CompilerParams(
        dimension_semantics=semantics, vmem_limit_bytes=VMEM_LIMIT_BYTES)


def _rmsnorm_kernel(x_ref, g_ref, o_ref):
    x = x_ref[...]
    ms = jnp.mean(x * x, axis=-1, keepdims=True)
    o_ref[...] = (x * lax.rsqrt(ms + EPS) * g_ref[...]).astype(o_ref.dtype)


def _rmsnorm(x, g, out_dtype, *, rows=NORM_ROWS):
    m, d = x.shape
    assert m % rows == 0 and d % LANE == 0, (x.shape, rows)
    return pl.pallas_call(
        _rmsnorm_kernel,
        grid=(m // rows,),
        in_specs=[pl.BlockSpec((rows, d), lambda i: (i, 0)),
                  pl.BlockSpec((1, d), lambda i: (0, 0))],
        out_specs=pl.BlockSpec((rows, d), lambda i: (i, 0)),
        out_shape=jax.ShapeDtypeStruct((m, d), out_dtype),
        compiler_params=_params("parallel"),
        name="rmsnorm",
    )(x, g.reshape(1, d))


def _mm_kernel(*refs, transposed_w, has_residual):
    x_ref, w_ref = refs[:2]
    o_ref = refs[-1]
    w = w_ref[...].astype(BF16)
    if transposed_w:
        acc = lax.dot_general(x_ref[...], w, _NT, preferred_element_type=F32)
    else:
        acc = jnp.dot(x_ref[...], w, preferred_element_type=F32)
    if has_residual:
        acc = refs[2][...] + acc
    o_ref[...] = acc.astype(o_ref.dtype)


def _mm(x, w, layer, n, out_dtype, *, tile, residual=None, transposed_w=False,
        single_buffer_x=False, name="mm"):
    m, k = x.shape
    tm, tn = tile
    k_axis, n_axis = (2, 1) if transposed_w else (1, 2)
    assert w.shape[k_axis] == k and w.shape[n_axis] >= n, (x.shape, w.shape, n)
    assert m % tm == 0 and n % tn == 0, (m, n, tm, tn)
    x_mode = dict(pipeline_mode=pl.Buffered(1)) if single_buffer_x else {}
    if transposed_w:
        w_spec = pl.BlockSpec((None, tn, k), lambda i, j: (layer, j, 0))
    else:
        w_spec = pl.BlockSpec((None, k, tn), lambda i, j: (layer, 0, j))
    in_specs = [pl.BlockSpec((tm, k), lambda i, j: (i, 0), **x_mode), w_spec]
    args = [x, w]
    if residual is not None:
        in_specs.append(pl.BlockSpec((tm, tn), lambda i, j: (i, j)))
        args.append(residual)
    return pl.pallas_call(
        functools.partial(_mm_kernel, transposed_w=transposed_w,
                          has_residual=residual is not None),
        grid=(m // tm, n // tn),
        in_specs=in_specs,
        out_specs=pl.BlockSpec((tm, tn), lambda i, j: (i, j)),
        out_shape=jax.ShapeDtypeStruct((m, n), out_dtype),
        compiler_params=_params("parallel", "arbitrary"),
        name=name,
    )(*args)


def _gate_kernel(x_ref, wt_ref, b_ref, o_ref, carry_ref, *, ts):
    c = pl.program_id(1)

    @pl.when(c == 0)
    def _():
        carry_ref[...] = jnp.zeros_like(carry_ref)

    f = lax.dot_general(wt_ref[...].astype(BF16), x_ref[...], _NT,
                        preferred_element_type=F32) + b_ref[...]
    s = jnp.minimum(f, 0.0) - jnp.log1p(jnp.exp(-jnp.abs(f)))
    lane = lax.broadcasted_iota(jnp.int32, s.shape, 1)
    shift = 1
    while shift < ts:
        s = s + jnp.where(lane >= shift, pltpu.roll(s, shift, axis=1), 0.0)
        shift *= 2
    s = s + carry_ref[...]
    o_ref[0] = s
    carry_ref[...] = s[:, ts - 1:ts]


def _fox_gate(hn, w_in_t, layer, gate_row, b, *, batch, seq, ts=GATE_SEQ_CHUNK):
    heads = b.shape[0]
    d = hn.shape[1]
    assert seq % ts == 0 and heads % SUBLANE == 0, (seq, ts, heads)
    assert gate_row % heads == 0 and gate_row + heads <= w_in_t.shape[1], (gate_row, w_in_t.shape)
    nchunk = seq // ts
    return pl.pallas_call(
        functools.partial(_gate_kernel, ts=ts),
        grid=(batch, nchunk),
        in_specs=[pl.BlockSpec((ts, d), lambda bi, c: (bi * nchunk + c, 0)),
                  pl.BlockSpec((None, heads, d), lambda bi, c: (layer, gate_row // heads, 0)),
                  pl.BlockSpec((heads, 1), lambda bi, c: (0, 0))],
        out_specs=pl.BlockSpec((1, heads, ts), lambda bi, c: (bi, 0, c)),
        out_shape=jax.ShapeDtypeStruct((batch, heads, seq), F32),
        scratch_shapes=[pltpu.VMEM((heads, 1), F32)],
        compiler_params=_params("parallel", "arbitrary"),
        name="fox_gate",
    )(hn, w_in_t, b.reshape(heads, 1))


def _fox_kernel(q_ref, k_ref, v_ref, cq_ref, ck_ref, o_ref, cs_ref, m_ref, l_ref, acc_ref,
                *, tq, nheads, scale):
    i = pl.program_id(2)
    d = FOX_HEAD_DIM
    nchunk = ck_ref.shape[2]

    @pl.when(i == 0)
    def _():
        r = lax.broadcasted_iota(jnp.int32, (LANE, LANE), 0)
        c = lax.broadcasted_iota(jnp.int32, (LANE, LANE), 1)
        for hh in range(nheads):
            def chunk(t, carry, hh=hh):
                row = ck_ref[0, hh, pl.ds(t, 1), :] * LOG2E
                col = jnp.sum(jnp.where(r == c, row, 0.0), axis=1, keepdims=True)
                cs_ref[hh, pl.ds(pl.multiple_of(t * LANE, LANE), LANE), :] = (
                    jnp.broadcast_to(col, (LANE, LANE)))
                return carry
            lax.fori_loop(0, nchunk, chunk, 0, unroll=8)

    m_ref[...] = jnp.full_like(m_ref, -jnp.inf)
    l_ref[...] = jnp.zeros_like(l_ref)
    acc_ref[...] = jnp.zeros_like(acc_ref)

    def step(j, masked):
        start = pl.multiple_of(j * tq, tq)
        heads = range(nheads)
        cols = [slice(hh * d, (hh + 1) * d) for hh in heads]
        scores = [lax.dot_general(k_ref[pl.ds(start, tq), cols[hh]], q_ref[:, cols[hh]], _NT,
                                  preferred_element_type=F32) for hh in heads]
        ys, shifts, alphas = [], [], []
        for hh in heads:
            ct = cq_ref[0, hh:hh + 1, :] * LOG2E
            cs = jnp.tile(cs_ref[hh, pl.ds(start, tq), :], (1, tq // LANE))
            y = scores[hh] * (scale * LOG2E) - cs
            if masked:
                r = lax.broadcasted_iota(jnp.int32, y.shape, 0)
                c = lax.broadcasted_iota(jnp.int32, y.shape, 1)
                y = jnp.where(c >= r, y, NEG)
            m_prev = m_ref[hh]
            m_new = jnp.maximum(m_prev, jnp.max(y, axis=0, keepdims=True) + ct)
            m_ref[hh] = m_new
            ys.append(y)
            shifts.append(m_new - ct)
            alphas.append(jnp.exp2(m_prev - m_new))
        for hh in heads:
            p = jnp.exp2(ys[hh] - shifts[hh])
            l_ref[hh] = alphas[hh] * l_ref[hh] + jnp.sum(p, axis=0, keepdims=True)
            pv = lax.dot_general(v_ref[pl.ds(start, tq), cols[hh]], p.astype(BF16),
                                 (((0,), (0,)), ((), ())),
                                 preferred_element_type=F32)
            acc_ref[hh] = alphas[hh] * acc_ref[hh] + pv

    def body(j, carry):
        step(j, masked=False)
        return carry

    lax.fori_loop(0, i, body, 0)
    step(i, masked=True)
    for hh in range(nheads):
        out = acc_ref[hh] / l_ref[hh]
        o_ref[:, hh * d:(hh + 1) * d] = out.T.astype(o_ref.dtype)


def _fox_attention(qkv, cum, *, batch, seq, heads, tq=FOX_Q_TILE, nheads=FOX_HEADS_PER_STEP):
    d = FOX_HEAD_DIM
    assert seq % tq == 0 and tq % LANE == 0 and heads % nheads == 0, (seq, tq, heads)
    assert qkv.shape == (batch * seq, 3 * heads * d), qkv.shape
    nq = seq // tq
    groups = heads // nheads
    w = nheads * d
    cum_q = cum.reshape(batch * groups, nheads, seq)
    cum_k = cum.reshape(batch * groups, nheads, seq // LANE, LANE)
    return pl.pallas_call(
        functools.partial(_fox_kernel, tq=tq, nheads=nheads, scale=d ** -0.5),
        grid=(batch, groups, nq),
        in_specs=[
            pl.BlockSpec((tq, w), lambda b, g, i: (b * nq + i, g)),
            pl.BlockSpec((seq, w), lambda b, g, i: (b, groups + g)),
            pl.BlockSpec((seq, w), lambda b, g, i: (b, 2 * groups + g)),
            pl.BlockSpec((1, nheads, tq), lambda b, g, i: (b * groups + g, 0, i)),
            pl.BlockSpec((1, nheads, seq // LANE, LANE),
                         lambda b, g, i: (b * groups + g, 0, 0, 0)),
        ],
        out_specs=pl.BlockSpec((tq, w), lambda b, g, i: (b * nq + i, g)),
        out_shape=jax.ShapeDtypeStruct((batch * seq, heads * d), BF16),
        scratch_shapes=[pltpu.VMEM((nheads, seq, LANE), F32),
                        pltpu.VMEM((nheads, 1, tq), F32),
                        pltpu.VMEM((nheads, 1, tq), F32),
                        pltpu.VMEM((nheads, d, tq), F32)],
        compiler_params=_params("parallel", "parallel", "arbitrary"),
        name="fox_attention",
    )(qkv, qkv, qkv, cum_q, cum_k)


def _swa_kernel(slope_ref, sink_ref, dist_ref, q_ref, kp_ref, kc_ref, vp_ref, vc_ref, o_ref,
                *, scale, pairs):
    half = SWA_HEAD_DIM
    npair = SWA_GROUP // 2
    nq = npair * Q_BLOCK
    lo = lax.broadcasted_iota(jnp.int32, (2 * Q_BLOCK, LANE), 1) < half
    dist = dist_ref[...]
    lane_pair = lax.broadcasted_iota(jnp.int32, (1, nq), 1) // Q_BLOCK

    groups = []
    for pp in range(pairs):
        lanes = slice(pp * LANE, (pp + 1) * LANE)
        kband = jnp.concatenate([kp_ref[:, lanes], kc_ref[:, lanes]], axis=0).astype(F32) * scale
        vband = jnp.concatenate([vp_ref[:, lanes], vc_ref[:, lanes]], axis=0).astype(F32)
        kroll = pltpu.roll(kband, half, axis=1)
        vroll = pltpu.roll(vband, half, axis=1)
        for cc in range(2):
            k_src, k_alt = (kband, kroll) if cc == 0 else (kroll, kband)
            v_src, v_alt = (vband, vroll) if cc == 0 else (vroll, vband)
            k_sides = (jnp.where(lo, k_src, 0.0).astype(BF16),
                       jnp.where(lo, 0.0, k_alt).astype(BF16))
            v_sides = (jnp.where(lo, v_src, 0.0).astype(BF16),
                       jnp.where(lo, 0.0, v_alt).astype(BF16))
            kv_head = 2 * pp + cc
            qs = jnp.concatenate(
                [q_ref[:, (kv_head * npair + t) * LANE:(kv_head * npair + t + 1) * LANE]
                 for t in range(npair)], axis=0)
            scores = [lax.dot_general(k_sides[par], qs, _NT, preferred_element_type=F32)
                      for par in range(2)]
            groups.append((kv_head, scores, v_sides))

    first_head = pl.program_id(2) * (2 * pairs * SWA_GROUP)
    for kv_head, scores, v_sides in groups:
        out_t = None
        for par in range(2):
            slope = jnp.zeros((1, nq), F32)
            sink = jnp.zeros((1, nq), F32)
            for t in range(npair):
                head = first_head + kv_head * SWA_GROUP + 2 * t + par
                slope = jnp.where(lane_pair == t, slope_ref[head], slope)
                sink = jnp.where(lane_pair == t, sink_ref[head], sink)
            s = scores[par] - slope * dist
            m = jnp.maximum(jnp.max(s, axis=0, keepdims=True), sink)
            e = jnp.exp(s - m)
            denom = jnp.sum(e, axis=0, keepdims=True) + jnp.exp(sink - m)
            p = (e * (1.0 / denom)).astype(BF16)
            pv = lax.dot_general(v_sides[par], p, (((0,), (0,)), ((), ())),
                                 preferred_element_type=F32)
            out_t = pv if out_t is None else out_t + pv
        out = out_t.T
        for t in range(npair):
            o_ref[:, (kv_head * npair + t) * LANE:(kv_head * npair + t + 1) * LANE] = (
                out[t * Q_BLOCK:(t + 1) * Q_BLOCK].astype(o_ref.dtype))


def _swa_attention(proj, slopes, sinks, *, batch, seq, d_model, max_pairs=SWA_MAX_PAIRS_PER_STEP):
    nb = seq // Q_BLOCK
    kv_heads = d_model // SWA_HEAD_DIM // SWA_GROUP
    pairs = min(max_pairs, kv_heads // 2)
    assert seq % Q_BLOCK == 0 and kv_heads % (2 * pairs) == 0, (seq, kv_heads, pairs)
    assert proj.shape == (batch * seq, d_model + 2 * kv_heads * SWA_HEAD_DIM), proj.shape
    steps = kv_heads // (2 * pairs)
    qw = 2 * pairs * SWA_GROUP * SWA_HEAD_DIM
    kw = pairs * LANE
    k0 = d_model // kw
    v0 = k0 + steps
    prev = lambda b, n, p: b * nb + jnp.maximum(n - 1, 0)
    smem = pl.BlockSpec(memory_space=pltpu.SMEM)
    band = 2 * Q_BLOCK
    nq = (SWA_GROUP // 2) * Q_BLOCK
    return pl.pallas_call(
        functools.partial(_swa_kernel, scale=SWA_HEAD_DIM ** -0.5, pairs=pairs),
        grid=(batch, nb, steps),
        in_specs=[
            smem, smem,
            pl.BlockSpec((None, band, nq), lambda b, n, p: (jnp.minimum(n, 1), 0, 0)),
            pl.BlockSpec((Q_BLOCK, qw), lambda b, n, p: (b * nb + n, p)),
            pl.BlockSpec((Q_BLOCK, kw), lambda b, n, p: (prev(b, n, p), k0 + p)),
            pl.BlockSpec((Q_BLOCK, kw), lambda b, n, p: (b * nb + n, k0 + p)),
            pl.BlockSpec((Q_BLOCK, kw), lambda b, n, p: (prev(b, n, p), v0 + p)),
            pl.BlockSpec((Q_BLOCK, kw), lambda b, n, p: (b * nb + n, v0 + p)),
        ],
        out_specs=pl.BlockSpec((Q_BLOCK, qw), lambda b, n, p: (b * nb + n, p)),
        out_shape=jax.ShapeDtypeStruct((batch * seq, d_model), BF16),
        compiler_params=_params("parallel", "parallel", "arbitrary"),
        name="swa_attention",
    )(slopes, sinks, _swa_distance_table(), proj, proj, proj, proj, proj)


def _swa_distance_table():
    band = 2 * Q_BLOCK
    key = jnp.arange(band)[:, None]
    query = jnp.arange((SWA_GROUP // 2) * Q_BLOCK)[None, :] % Q_BLOCK
    dist = jnp.abs(query - (key - Q_BLOCK)).astype(F32)
    q_chunk = query // CHUNK
    k_chunk = key // CHUNK - WINDOW_CHUNKS
    window = (k_chunk <= q_chunk) & (k_chunk >= q_chunk - WINDOW_CHUNKS)
    first = window & (key >= Q_BLOCK)
    return jnp.stack([jnp.where(first, dist, MASKED_DIST), jnp.where(window, dist, MASKED_DIST)])


def _up_kernel(x_ref, wg_ref, wv_ref, cwg_ref, cwv_ref, cbg_ref, cbv_ref, o_ref,
               ug_ref, uv_ref, tailg_ref, tailv_ref, *, tm, seq_tiles):
    i = pl.program_id(0)
    j = pl.program_id(1)
    x = x_ref[...]
    seq_start = (i % seq_tiles) == 0

    def conv(w_ref, u_ref, tail_ref, cw_ref, cb_ref):
        u_ref[0:SUBLANE, :] = jnp.where(seq_start, 0.0, tail_ref[j])
        u_ref[SUBLANE:SUBLANE + tm, :] = jnp.dot(
            x, w_ref[...].astype(BF16), preferred_element_type=F32)
        tail_ref[j] = u_ref[tm:tm + SUBLANE, :]
        cw = cw_ref[...]
        out = cb_ref[...]
        for tap in range(CONV_WIDTH):
            off = SUBLANE - (CONV_WIDTH - 1) + tap
            out = out + cw[tap:tap + 1, :] * u_ref[off:off + tm, :]
        return out

    gate = conv(wg_ref, ug_ref, tailg_ref, cwg_ref, cbg_ref)
    val = conv(wv_ref, uv_ref, tailv_ref, cwv_ref, cbv_ref)
    o_ref[...] = (gate * (1.0 / (1.0 + jnp.exp2(gate * -LOG2E))) * val).astype(o_ref.dtype)


def _conv_ffn_up(hn, w_up, conv_w, conv_b, layer, *, seq, tm=FFN_UP_TILE[0], tn=FFN_UP_TILE[1]):
    m, d = hn.shape
    f = w_up.shape[2] // 2
    assert m % tm == 0 and seq % tm == 0 and f % tn == 0, (m, seq, f, tm, tn)
    nj = f // tn
    conv_b = conv_b.reshape(conv_b.shape[0], 1, 2 * f)
    gate = lambda i, j: (layer, 0, j)
    val = lambda i, j: (layer, 0, nj + j)
    slot = pltpu.VMEM((tm + SUBLANE, tn), F32)
    tail = pltpu.VMEM((nj, SUBLANE, tn), F32)
    return pl.pallas_call(
        functools.partial(_up_kernel, tm=tm, seq_tiles=seq // tm),
        grid=(m // tm, nj),
        in_specs=[
            pl.BlockSpec((tm, d), lambda i, j: (i, 0)),
            pl.BlockSpec((None, d, tn), gate),
            pl.BlockSpec((None, d, tn), val),
            pl.BlockSpec((None, CONV_WIDTH, tn), gate),
            pl.BlockSpec((None, CONV_WIDTH, tn), val),
            pl.BlockSpec((None, 1, tn), gate),
            pl.BlockSpec((None, 1, tn), val),
        ],
        out_specs=pl.BlockSpec((tm, tn), lambda i, j: (i, j)),
        out_shape=jax.ShapeDtypeStruct((m, f), BF16),
        scratch_shapes=[slot, slot, tail, tail],
        compiler_params=_params("arbitrary", "arbitrary"),
        name="conv_ffn_up",
    )(hn, w_up, w_up, conv_w, conv_w, conv_b, conv_b)


def kernel(x, attn_norm_g, fox_w_in, fox_b_f, fox_w_o, swa_w_in, swa_sinks, swa_w_o,
           ffn_norm_g, ffn_w_up, ffn_conv_w, ffn_conv_b, ffn_w_down, final_norm_g):
    batch, seq, d_model = x.shape
    depth = attn_norm_g.shape[0]
    fox_heads = fox_b_f.shape[1]
    swa_heads = swa_sinks.shape[1]
    slopes = jnp.exp2(-8.0 * jnp.arange(1, swa_heads + 1, dtype=F32) / swa_heads)

    fox_w_in_t = jnp.swapaxes(fox_w_in, 1, 2)

    h = x.reshape(batch * seq, d_model)
    for layer in range(depth):
        inst = layer // 2
        hn = _rmsnorm(h, attn_norm_g[layer], BF16)
        if layer % 2 == 0:
            qkv = _mm(hn, fox_w_in_t, inst, 3 * d_model, BF16, tile=QKV_TILE,
                      transposed_w=True, single_buffer_x=True, name="fox_qkv")
            cum = _fox_gate(hn, fox_w_in_t, inst, 3 * d_model, fox_b_f[inst],
                            batch=batch, seq=seq)
            mix = _fox_attention(qkv, cum, batch=batch, seq=seq, heads=fox_heads)
            w_o = fox_w_o
        else:
            proj = _mm(hn, swa_w_in, inst, swa_w_in.shape[2], BF16, tile=QKV_TILE,
                       single_buffer_x=True, name="swa_qkv")
            mix = _swa_attention(proj, slopes, swa_sinks[inst],
                                 batch=batch, seq=seq, d_model=d_model)
            w_o = swa_w_o
        h = _mm(mix, w_o, inst, d_model, F32, tile=ATTN_OUT_TILE, residual=h, name="attn_out")
        hn = _rmsnorm(h, ffn_norm_g[layer], BF16)
        act = _conv_ffn_up(hn, ffn_w_up, ffn_conv_w, ffn_conv_b, layer, seq=seq)
        h = _mm(act, ffn_w_down, layer, d_model, F32, tile=FFN_DOWN_TILE,
                residual=h, single_buffer_x=True, name="ffn_down")
    out = _rmsnorm(h, final_norm_g, F32)
    return out.reshape(batch, seq, d_model)
```

```python
import functools

import jax
import jax.numpy as jnp
from jax import lax
from jax.experimental import pallas as pl
from jax.experimental.pallas import tpu as pltpu

LANE = 128
SUBLANE = 8
VMEM_LIMIT_BYTES = 56 * 1024 * 1024

FOX_HEAD_DIM = 128
SWA_HEAD_DIM = 64
SWA_GROUP = 8
Q_BLOCK = 128
CHUNK = 64
WINDOW_CHUNKS = 2
CONV_WIDTH = 3
EPS = 1e-6
NEG = -1e30
MASKED_DIST = 1e30
LOG2E = 1.4426950408889634

BF16 = jnp.bfloat16
F32 = jnp.float32

QKV_TILE = (2048, 512)
ATTN_OUT_TILE = (1024, 512)
FFN_UP_TILE = (1024, 256)
FFN_DOWN_TILE = (1024, 256)
NORM_ROWS = 512
GATE_SEQ_CHUNK = 1024
FOX_Q_TILE = 512
FOX_HEADS_PER_STEP = 4
SWA_MAX_PAIRS_PER_STEP = 4

_NT = (((1,), (1,)), ((), ()))


def _params(*semantics):
    return pltpu.CompilerParams(
        dimension_semantics=semantics, vmem_limit_bytes=VMEM_LIMIT_BYTES)


def _rmsnorm_kernel(x_ref, g_ref, o_ref):
    x = x_ref[...]
    ms = jnp.mean(x * x, axis=-1, keepdims=True)
    o_ref[...] = (x * lax.rsqrt(ms + EPS) * g_ref[...]).astype(o_ref.dtype)


def _rmsnorm(x, g, out_dtype, *, rows=NORM_ROWS):
    m, d = x.shape
    assert m % rows == 0 and d % LANE == 0, (x.shape, rows)
    return pl.pallas_call(
        _rmsnorm_kernel,
        grid=(m // rows,),
        in_specs=[pl.BlockSpec((rows, d), lambda i: (i, 0)),
                  pl.BlockSpec((1, d), lambda i: (0, 0))],
        out_specs=pl.BlockSpec((rows, d), lambda i: (i, 0)),
        out_shape=jax.ShapeDtypeStruct((m, d), out_dtype),
        compiler_params=_params("parallel"),
        name="rmsnorm",
    )(x, g.reshape(1, d))


def _mm_kernel(*refs, transposed_w, has_residual):
    x_ref, w_ref = refs[:2]
    o_ref = refs[-1]
    w = w_ref[...].astype(BF16)
    if transposed_w:
        acc = lax.dot_general(x_ref[...], w, _NT, preferred_element_type=F32)
    else:
        acc = jnp.dot(x_ref[...], w, preferred_element_type=F32)
    if has_residual:
        acc = refs[2][...] + acc
    o_ref[...] = acc.astype(o_ref.dtype)


def _mm(x, w, layer, n, out_dtype, *, tile, residual=None, transposed_w=False,
        single_buffer_x=False, name="mm"):
    m, k = x.shape
    tm, tn = tile
    k_axis, n_axis = (2, 1) if transposed_w else (1, 2)
    assert w.shape[k_axis] == k and w.shape[n_axis] >= n, (x.shape, w.shape, n)
    assert m % tm == 0 and n % tn == 0, (m, n, tm, tn)
    x_mode = dict(pipeline_mode=pl.Buffered(1)) if single_buffer_x else {}
    if transposed_w:
        w_spec = pl.BlockSpec((None, tn, k), lambda i, j: (layer, j, 0))
    else:
        w_spec = pl.BlockSpec((None, k, tn), lambda i, j: (layer, 0, j))
    in_specs = [pl.BlockSpec((tm, k), lambda i, j: (i, 0), **x_mode), w_spec]
    args = [x, w]
    if residual is not None:
        in_specs.append(pl.BlockSpec((tm, tn), lambda i, j: (i, j)))
        args.append(residual)
    return pl.pallas_call(
        functools.partial(_mm_kernel, transposed_w=transposed_w,
                          has_residual=residual is not None),
        grid=(m // tm, n // tn),
        in_specs=in_specs,
        out_specs=pl.BlockSpec((tm, tn), lambda i, j: (i, j)),
        out_shape=jax.ShapeDtypeStruct((m, n), out_dtype),
        compiler_params=_params("parallel", "arbitrary"),
        name=name,
    )(*args)


def _gate_kernel(x_ref, wt_ref, b_ref, o_ref, carry_ref, *, ts):
    c = pl.program_id(1)

    @pl.when(c == 0)
    def _():
        carry_ref[...] = jnp.zeros_like(carry_ref)

    f = lax.dot_general(wt_ref[...].astype(BF16), x_ref[...], _NT,
                        preferred_element_type=F32) + b_ref[...]
    s = jnp.minimum(f, 0.0) - jnp.log1p(jnp.exp(-jnp.abs(f)))
    lane = lax.broadcasted_iota(jnp.int32, s.shape, 1)
    shift = 1
    while shift < ts:
        s = s + jnp.where(lane >= shift, pltpu.roll(s, shift, axis=1), 0.0)
        shift *= 2
    s = s + carry_ref[...]
    o_ref[0] = s
    carry_ref[...] = s[:, ts - 1:ts]


def _fox_gate(hn, w_in_t, layer, gate_row, b, *, batch, seq, ts=GATE_SEQ_CHUNK):
    heads = b.shape[0]
    d = hn.shape[1]
    assert seq % ts == 0 and heads % SUBLANE == 0, (seq, ts, heads)
    assert gate_row % heads == 0 and gate_row + heads <= w_in_t.shape[1], (gate_row, w_in_t.shape)
    nchunk = seq // ts
    return pl.pallas_call(
        functools.partial(_gate_kernel, ts=ts),
        grid=(batch, nchunk),
        in_specs=[pl.BlockSpec((ts, d), lambda bi, c: (bi * nchunk + c, 0)),
                  pl.BlockSpec((None, heads, d), lambda bi, c: (layer, gate_row // heads, 0)),
                  pl.BlockSpec((heads, 1), lambda bi, c: (0, 0))],
        out_specs=pl.BlockSpec((1, heads, ts), lambda bi, c: (bi, 0, c)),
        out_shape=jax.ShapeDtypeStruct((batch, heads, seq), F32),
        scratch_shapes=[pltpu.VMEM((heads, 1), F32)],
        compiler_params=_params("parallel", "arbitrary"),
        name="fox_gate",
    )(hn, w_in_t, b.reshape(heads, 1))


def _fox_kernel(q_ref, k_ref, v_ref, cq_ref, ck_ref, o_ref, cs_ref, m_ref, l_ref, acc_ref,
                *, tq, nheads, scale):
    i = pl.program_id(2)
    d = FOX_HEAD_DIM
    nchunk = ck_ref.shape[2]

    @pl.when(i == 0)
    def _():
        r = lax.broadcasted_iota(jnp.int32, (LANE, LANE), 0)
        c = lax.broadcasted_iota(jnp.int32, (LANE, LANE), 1)
        for hh in range(nheads):
            def chunk(t, carry, hh=hh):
                row = ck_ref[0, hh, pl.ds(t, 1), :] * LOG2E
                col = jnp.sum(jnp.where(r == c, row, 0.0), axis=1, keepdims=True)
                cs_ref[hh, pl.ds(pl.multiple_of(t * LANE, LANE), LANE), :] = (
                    jnp.broadcast_to(col, (LANE, LANE)))
                return carry
            lax.fori_loop(0, nchunk, chunk, 0, unroll=8)

    m_ref[...] = jnp.full_like(m_ref, -jnp.inf)
    l_ref[...] = jnp.zeros_like(l_ref)
    acc_ref[...] = jnp.zeros_like(acc_ref)

    def step(j, masked):
        start = pl.multiple_of(j * tq, tq)
        heads = range(nheads)
        cols = [slice(hh * d, (hh + 1) * d) for hh in heads]
        scores = [lax.dot_general(k_ref[pl.ds(start, tq), cols[hh]], q_ref[:, cols[hh]], _NT,
                                  preferred_element_type=F32) for hh in heads]
        ys, shifts, alphas = [], [], []
        for hh in heads:
            ct = cq_ref[0, hh:hh + 1, :] * LOG2E
            cs = jnp.tile(cs_ref[hh, pl.ds(start, tq), :], (1, tq // LANE))
            y = scores[hh] * (scale * LOG2E) - cs
            if masked:
                r = lax.broadcasted_iota(jnp.int32, y.shape, 0)
                c = lax.broadcasted_iota(jnp.int32, y.shape, 1)
                y = jnp.where(c >= r, y, NEG)
            m_prev = m_ref[hh]
            m_new = jnp.maximum(m_prev, jnp.max(y, axis=0, keepdims=True) + ct)
            m_ref[hh] = m_new
            ys.append(y)
            shifts.append(m_new - ct)
            alphas.append(jnp.exp2(m_prev - m_new))
        for hh in heads:
            p = jnp.exp2(ys[hh] - shifts[hh])
            l_ref[hh] = alphas[hh] * l_ref[hh] + jnp.sum(p, axis=0, keepdims=True)
            pv = lax.dot_general(v_ref[pl.ds(start, tq), cols[hh]], p.astype(BF16),
                                 (((0,), (0,)), ((), ())),
                                 preferred_element_type=F32)
            acc_ref[hh] = alphas[hh] * acc_ref[hh] + pv

    def body(j, carry):
        step(j, masked=False)
        return carry

    lax.fori_loop(0, i, body, 0)
    step(i, masked=True)
    for hh in range(nheads):
        out = acc_ref[hh] / l_ref[hh]
        o_ref[:, hh * d:(hh + 1) * d] = out.T.astype(o_ref.dtype)


def _fox_attention(qkv, cum, *, batch, seq, heads, tq=FOX_Q_TILE, nheads=FOX_HEADS_PER_STEP):
    d = FOX_HEAD_DIM
    assert seq % tq == 0 and tq % LANE == 0 and heads % nheads == 0, (seq, tq, heads)
    assert qkv.shape == (batch * seq, 3 * heads * d), qkv.shape
    nq = seq // tq
    groups = heads // nheads
    w = nheads * d
    cum_q = cum.reshape(batch * groups, nheads, seq)
    cum_k = cum.reshape(batch * groups, nheads, seq // LANE, LANE)
    return pl.pallas_call(
        functools.partial(_fox_kernel, tq=tq, nheads=nheads, scale=d ** -0.5),
        grid=(batch, groups, nq),
        in_specs=[
            pl.BlockSpec((tq, w), lambda b, g, i: (b * nq + i, g)),
            pl.BlockSpec((seq, w), lambda b, g, i: (b, groups + g)),
            pl.BlockSpec((seq, w), lambda b, g, i: (b, 2 * groups + g)),
            pl.BlockSpec((1, nheads, tq), lambda b, g, i: (b * groups + g, 0, i)),
            pl.BlockSpec((1, nheads, seq // LANE, LANE),
                         lambda b, g, i: (b * groups + g, 0, 0, 0)),
        ],
        out_specs=pl.BlockSpec((tq, w), lambda b, g, i: (b * nq + i, g)),
        out_shape=jax.ShapeDtypeStruct((batch * seq, heads * d), BF16),
        scratch_shapes=[pltpu.VMEM((nheads, seq, LANE), F32),
                        pltpu.VMEM((nheads, 1, tq), F32),
                        pltpu.VMEM((nheads, 1, tq), F32),
                        pltpu.VMEM((nheads, d, tq), F32)],
        compiler_params=_params("parallel", "parallel", "arbitrary"),
        name="fox_attention",
    )(qkv, qkv, qkv, cum_q, cum_k)


def _swa_kernel(slope_ref, sink_ref, dist_ref, q_ref, kp_ref, kc_ref, vp_ref, vc_ref, o_ref,
                *, scale, pairs):
    half = SWA_HEAD_DIM
    npair = SWA_GROUP // 2
    nq = npair * Q_BLOCK
    lo = lax.broadcasted_iota(jnp.int32, (2 * Q_BLOCK, LANE), 1) < half
    dist = dist_ref[...]
    lane_pair = lax.broadcasted_iota(jnp.int32, (1, nq), 1) // Q_BLOCK

    groups = []
    for pp in range(pairs):
        lanes = slice(pp * LANE, (pp + 1) * LANE)
        kband = jnp.concatenate([kp_ref[:, lanes], kc_ref[:, lanes]], axis=0).astype(F32) * scale
        vband = jnp.concatenate([vp_ref[:, lanes], vc_ref[:, lanes]], axis=0).astype(F32)
        kroll = pltpu.roll(kband, half, axis=1)
        vroll = pltpu.roll(vband, half, axis=1)
        for cc in range(2):
            k_src, k_alt = (kband, kroll) if cc == 0 else (kroll, kband)
            v_src, v_alt = (vband, vroll) if cc == 0 else (vroll, vband)
            k_sides = (jnp.where(lo, k_src, 0.0).astype(BF16),
                       jnp.where(lo, 0.0, k_alt).astype(BF16))
            v_sides = (jnp.where(lo, v_src, 0.0).astype(BF16),
                       jnp.where(lo, 0.0, v_alt).astype(BF16))
            kv_head = 2 * pp + cc
            qs = jnp.concatenate(
                [q_ref[:, (kv_head * npair + t) * LANE:(kv_head * npair + t + 1) * LANE]
                 for t in range(npair)], axis=0)
            scores = [lax.dot_general(k_sides[par], qs, _NT, preferred_element_type=F32)
                      for par in range(2)]
            groups.append((kv_head, scores, v_sides))

    first_head = pl.program_id(2) * (2 * pairs * SWA_GROUP)
    for kv_head, scores, v_sides in groups:
        out_t = None
        for par in range(2):
            slope = jnp.zeros((1, nq), F32)
            sink = jnp.zeros((1, nq), F32)
            for t in range(npair):
                head = first_head + kv_head * SWA_GROUP + 2 * t + par
                slope = jnp.where(lane_pair == t, slope_ref[head], slope)
                sink = jnp.where(lane_pair == t, sink_ref[head], sink)
            s = scores[par] - slope * dist
            m = jnp.maximum(jnp.max(s, axis=0, keepdims=True), sink)
            e = jnp.exp(s - m)
            denom = jnp.sum(e, axis=0, keepdims=True) + jnp.exp(sink - m)
            p = (e * (1.0 / denom)).astype(BF16)
            pv = lax.dot_general(v_sides[par], p, (((0,), (0,)), ((), ())),
                                 preferred_element_type=F32)
            out_t = pv if out_t is None else out_t + pv
        out = out_t.T
        for t in range(npair):
            o_ref[:, (kv_head * npair + t) * LANE:(kv_head * npair + t + 1) * LANE] = (
                out[t * Q_BLOCK:(t + 1) * Q_BLOCK].astype(o_ref.dtype))


def _swa_attention(proj, slopes, sinks, *, batch, seq, d_model, max_pairs=SWA_MAX_PAIRS_PER_STEP):
    nb = seq // Q_BLOCK
    kv_heads = d_model // SWA_HEAD_DIM // SWA_GROUP
    pairs = min(max_pairs, kv_heads // 2)
    assert seq % Q_BLOCK == 0 and kv_heads % (2 * pairs) == 0, (seq, kv_heads, pairs)
    assert proj.shape == (batch * seq, d_model + 2 * kv_heads * SWA_HEAD_DIM), proj.shape
    steps = kv_heads // (2 * pairs)
    qw = 2 * pairs * SWA_GROUP * SWA_HEAD_DIM
    kw = pairs * LANE
    k0 = d_model // kw
    v0 = k0 + steps
    prev = lambda b, n, p: b * nb + jnp.maximum(n - 1, 0)
    smem = pl.BlockSpec(memory_space=pltpu.SMEM)
    band = 2 * Q_BLOCK
    nq = (SWA_GROUP // 2) * Q_BLOCK
    return pl.pallas_call(
        functools.partial(_swa_kernel, scale=SWA_HEAD_DIM ** -0.5, pairs=pairs),
        grid=(batch, nb, steps),
        in_specs=[
            smem, smem,
            pl.BlockSpec((None, band, nq), lambda b, n, p: (jnp.minimum(n, 1), 0, 0)),
            pl.BlockSpec((Q_BLOCK, qw), lambda b, n, p: (b * nb + n, p)),
            pl.BlockSpec((Q_BLOCK, kw), lambda b, n, p: (prev(b, n, p), k0 + p)),
            pl.BlockSpec((Q_BLOCK, kw), lambda b, n, p: (b * nb + n, k0 + p)),
            pl.BlockSpec((Q_BLOCK, kw), lambda b, n, p: (prev(b, n, p), v0 + p)),
            pl.BlockSpec((Q_BLOCK, kw), lambda b, n, p: (b * nb + n, v0 + p)),
        ],
        out_specs=pl.BlockSpec((Q_BLOCK, qw), lambda b, n, p: (b * nb + n, p)),
        out_shape=jax.ShapeDtypeStruct((batch * seq, d_model), BF16),
        compiler_params=_params("parallel", "parallel", "arbitrary"),
        name="swa_attention",
    )(slopes, sinks, _swa_distance_table(), proj, proj, proj, proj, proj)


def _swa_distance_table():
    band = 2 * Q_BLOCK
    key = jnp.arange(band)[:, None]
    query = jnp.arange((SWA_GROUP // 2) * Q_BLOCK)[None, :] % Q_BLOCK
    dist = jnp.abs(query - (key - Q_BLOCK)).astype(F32)
    q_chunk = query // CHUNK
    k_chunk = key // CHUNK - WINDOW_CHUNKS
    window = (k_chunk <= q_chunk) & (k_chunk >= q_chunk - WINDOW_CHUNKS)
    first = window & (key >= Q_BLOCK)
    return jnp.stack([jnp.where(first, dist, MASKED_DIST), jnp.where(window, dist, MASKED_DIST)])


def _up_kernel(x_ref, wg_ref, wv_ref, cwg_ref, cwv_ref, cbg_ref, cbv_ref, o_ref,
               ug_ref, uv_ref, res_ref, tailg_ref, tailv_ref, *, tm, seq_tiles):
    i = pl.program_id(0)
    j = pl.program_id(1)
    x = x_ref[...]
    seq_start = (i % seq_tiles) == 0
    nslab = ug_ref.shape[0]
    half = tm // 2

    for w_ref, u_ref, tail_ref in ((wg_ref, ug_ref, tailg_ref), (wv_ref, uv_ref, tailv_ref)):
        u = jnp.dot(x, w_ref[...].astype(BF16), preferred_element_type=F32)
        for s in range(nslab):
            u_ref[s, 0:SUBLANE, :] = jnp.where(seq_start, 0.0, tail_ref[j, s])
            u_ref[s, SUBLANE:SUBLANE + tm, :] = u[:, s * LANE:(s + 1) * LANE]
            tail_ref[j, s] = u_ref[s, tm:tm + SUBLANE, :]

    def conv(u_ref, cw_ref, cb_ref, s, parity):
        lanes = slice(s * LANE, (s + 1) * LANE)
        cw = cw_ref[:, lanes]
        out = cb_ref[:, lanes]
        for tap in range(CONV_WIDTH):
            first = SUBLANE - (CONV_WIDTH - 1) + tap + parity
            out = out + cw[tap:tap + 1, :] * u_ref[s, pl.ds(first, half, stride=2), :]
        return out

    for s in range(nslab):
        for parity in range(2):
            gate = conv(ug_ref, cwg_ref, cbg_ref, s, parity)
            val = conv(uv_ref, cwv_ref, cbv_ref, s, parity)
            res_ref[s, pl.ds(parity, half, stride=2), :] = (
                gate * (1.0 / (1.0 + jnp.exp2(gate * -LOG2E))) * val)
        o_ref[:, s * LANE:(s + 1) * LANE] = res_ref[s].astype(o_ref.dtype)


def _conv_ffn_up(hn, w_up, conv_w, conv_b, layer, *, seq, tm=FFN_UP_TILE[0], tn=FFN_UP_TILE[1]):
    m, d = hn.shape
    f = w_up.shape[2] // 2
    assert m % tm == 0 and seq % tm == 0 and f % tn == 0 and tn % LANE == 0, (m, seq, f, tm, tn)
    nj = f // tn
    nslab = tn // LANE
    conv_b = conv_b.reshape(conv_b.shape[0], 1, 2 * f)
    gate = lambda i, j: (layer, 0, j)
    val = lambda i, j: (layer, 0, nj + j)
    slot = pltpu.VMEM((nslab, tm + SUBLANE, LANE), F32)
    tail = pltpu.VMEM((nj, nslab, SUBLANE, LANE), F32)
    return pl.pallas_call(
        functools.partial(_up_kernel, tm=tm, seq_tiles=seq // tm),
        grid=(m // tm, nj),
        in_specs=[
            pl.BlockSpec((tm, d), lambda i, j: (i, 0)),
            pl.BlockSpec((None, d, tn), gate),
            pl.BlockSpec((None, d, tn), val),
            pl.BlockSpec((None, CONV_WIDTH, tn), gate),
            pl.BlockSpec((None, CONV_WIDTH, tn), val),
            pl.BlockSpec((None, 1, tn), gate),
            pl.BlockSpec((None, 1, tn), val),
        ],
        out_specs=pl.BlockSpec((tm, tn), lambda i, j: (i, j)),
        out_shape=jax.ShapeDtypeStruct((m, f), BF16),
        scratch_shapes=[slot, slot, pltpu.VMEM((nslab, tm, LANE), F32), tail, tail],
        compiler_params=_params("arbitrary", "arbitrary"),
        name="conv_ffn_up",
    )(hn, w_up, w_up, conv_w, conv_w, conv_b, conv_b)


def kernel(x, attn_norm_g, fox_w_in, fox_b_f, fox_w_o, swa_w_in, swa_sinks, swa_w_o,
           ffn_norm_g, ffn_w_up, ffn_conv_w, ffn_conv_b, ffn_w_down, final_norm_g):
    batch, seq, d_model = x.shape
    depth = attn_norm_g.shape[0]
    fox_heads = fox_b_f.shape[1]
    swa_heads = swa_sinks.shape[1]
    slopes = jnp.exp2(-8.0 * jnp.arange(1, swa_heads + 1, dtype=F32) / swa_heads)

    fox_w_in_t = jnp.swapaxes(fox_w_in, 1, 2)

    h = x.reshape(batch * seq, d_model)
    for layer in range(depth):
        inst = layer // 2
        hn = _rmsnorm(h, attn_norm_g[layer], BF16)
        if layer % 2 == 0:
            qkv = _mm(hn, fox_w_in_t, inst, 3 * d_model, BF16, tile=QKV_TILE,
                      transposed_w=True, single_buffer_x=True, name="fox_qkv")
            cum = _fox_gate(hn, fox_w_in_t, inst, 3 * d_model, fox_b_f[inst],
                            batch=batch, seq=seq)
            mix = _fox_attention(qkv, cum, batch=batch, seq=seq, heads=fox_heads)
            w_o = fox_w_o
        else:
            proj = _mm(hn, swa_w_in, inst, swa_w_in.shape[2], BF16, tile=QKV_TILE,
                       single_buffer_x=True, name="swa_qkv")
            mix = _swa_attention(proj, slopes, swa_sinks[inst],
                                 batch=batch, seq=seq, d_model=d_model)
            w_o = swa_w_o
        h = _mm(mix, w_o, inst, d_model, F32, tile=ATTN_OUT_TILE, residual=h, name="attn_out")
        hn = _rmsnorm(h, ffn_norm_g[layer], BF16)
        act = _conv_ffn_up(hn, ffn_w_up, ffn_conv_w, ffn_conv_b, layer, seq=seq)
        h = _mm(act, ffn_w_down, layer, d_model, F32, tile=FFN_DOWN_TILE,
                residual=h, single_buffer_x=True, name="ffn_down")
    out = _rmsnorm(h, final_norm_g, F32)
    return out.reshape(batch, seq, d_model)
```

```python
import functools

import jax
import jax.numpy as jnp
from jax import lax
from jax.experimental import pallas as pl
from jax.experimental.pallas import tpu as pltpu

LANE = 128
SUBLANE = 8
VMEM_LIMIT_BYTES = 56 * 1024 * 1024

FOX_HEAD_DIM = 128
SWA_HEAD_DIM = 64
SWA_GROUP = 8
Q_BLOCK = 128
CHUNK = 64
WINDOW_CHUNKS = 2
CONV_WIDTH = 3
CONV_ROW_PHASES = 4
EPS = 1e-6
NEG = -1e30
MASKED_DIST = 1e30
LOG2E = 1.4426950408889634

BF16 = jnp.bfloat16
F32 = jnp.float32

QKV_TILE = (2048, 512)
ATTN_OUT_TILE = (1024, 512)
FFN_UP_TILE = (1024, 256)
FFN_DOWN_TILE = (1024, 256)
NORM_ROWS = 512
GATE_SEQ_CHUNK = 1024
FOX_Q_TILE = 512
FOX_HEADS_PER_STEP = 4
SWA_MAX_PAIRS_PER_STEP = 4

_NT = (((1,), (1,)), ((), ()))


def _params(*semantics):
    return pltpu.CompilerParams(
        dimension_semantics=semantics, vmem_limit_bytes=VMEM_LIMIT_BYTES)


def _rmsnorm_kernel(x_ref, g_ref, o_ref):
    x = x_ref[...]
    ms = jnp.mean(x * x, axis=-1, keepdims=True)
    o_ref[...] = (x * lax.rsqrt(ms + EPS) * g_ref[...]).astype(o_ref.dtype)


def _rmsnorm(x, g, out_dtype, *, rows=NORM_ROWS):
    m, d = x.shape
    assert m % rows == 0 and d % LANE == 0, (x.shape, rows)
    return pl.pallas_call(
        _rmsnorm_kernel,
        grid=(m // rows,),
        in_specs=[pl.BlockSpec((rows, d), lambda i: (i, 0)),
                  pl.BlockSpec((1, d), lambda i: (0, 0))],
        out_specs=pl.BlockSpec((rows, d), lambda i: (i, 0)),
        out_shape=jax.ShapeDtypeStruct((m, d), out_dtype),
        compiler_params=_params("parallel"),
        name="rmsnorm",
    )(x, g.reshape(1, d))


def _mm_kernel(*refs, transposed_w, has_residual):
    x_ref, w_ref = refs[:2]
    o_ref = refs[-1]
    w = w_ref[...].astype(BF16)
    if transposed_w:
        acc = lax.dot_general(x_ref[...], w, _NT, preferred_element_type=F32)
    else:
        acc = jnp.dot(x_ref[...], w, preferred_element_type=F32)
    if has_residual:
        acc = refs[2][...] + acc
    o_ref[...] = acc.astype(o_ref.dtype)


def _mm(x, w, layer, n, out_dtype, *, tile, residual=None, transposed_w=False,
        single_buffer_x=False, name="mm"):
    m, k = x.shape
    tm, tn = tile
    k_axis, n_axis = (2, 1) if transposed_w else (1, 2)
    assert w.shape[k_axis] == k and w.shape[n_axis] >= n, (x.shape, w.shape, n)
    assert m % tm == 0 and n % tn == 0, (m, n, tm, tn)
    x_mode = dict(pipeline_mode=pl.Buffered(1)) if single_buffer_x else {}
    if transposed_w:
        w_spec = pl.BlockSpec((None, tn, k), lambda i, j: (layer, j, 0))
    else:
        w_spec = pl.BlockSpec((None, k, tn), lambda i, j: (layer, 0, j))
    in_specs = [pl.BlockSpec((tm, k), lambda i, j: (i, 0), **x_mode), w_spec]
    args = [x, w]
    if residual is not None:
        in_specs.append(pl.BlockSpec((tm, tn), lambda i, j: (i, j)))
        args.append(residual)
    return pl.pallas_call(
        functools.partial(_mm_kernel, transposed_w=transposed_w,
                          has_residual=residual is not None),
        grid=(m // tm, n // tn),
        in_specs=in_specs,
        out_specs=pl.BlockSpec((tm, tn), lambda i, j: (i, j)),
        out_shape=jax.ShapeDtypeStruct((m, n), out_dtype),
        compiler_params=_params("parallel", "arbitrary"),
        name=name,
    )(*args)


def _gate_kernel(x_ref, wt_ref, b_ref, o_ref, carry_ref, *, ts):
    c = pl.program_id(1)

    @pl.when(c == 0)
    def _():
        carry_ref[...] = jnp.zeros_like(carry_ref)

    f = lax.dot_general(wt_ref[...].astype(BF16), x_ref[...], _NT,
                        preferred_element_type=F32) + b_ref[...]
    s = jnp.minimum(f, 0.0) - jnp.log1p(jnp.exp(-jnp.abs(f)))
    lane = lax.broadcasted_iota(jnp.int32, s.shape, 1)
    shift = 1
    while shift < ts:
        s = s + jnp.where(lane >= shift, pltpu.roll(s, shift, axis=1), 0.0)
        shift *= 2
    s = s + carry_ref[...]
    o_ref[0] = s
    carry_ref[...] = s[:, ts - 1:ts]


def _fox_gate(hn, w_in_t, layer, gate_row, b, *, batch, seq, ts=GATE_SEQ_CHUNK):
    heads = b.shape[0]
    d = hn.shape[1]
    assert seq % ts == 0 and heads % SUBLANE == 0, (seq, ts, heads)
    assert gate_row % heads == 0 and gate_row + heads <= w_in_t.shape[1], (gate_row, w_in_t.shape)
    nchunk = seq // ts
    return pl.pallas_call(
        functools.partial(_gate_kernel, ts=ts),
        grid=(batch, nchunk),
        in_specs=[pl.BlockSpec((ts, d), lambda bi, c: (bi * nchunk + c, 0)),
                  pl.BlockSpec((None, heads, d), lambda bi, c: (layer, gate_row // heads, 0)),
                  pl.BlockSpec((heads, 1), lambda bi, c: (0, 0))],
        out_specs=pl.BlockSpec((1, heads, ts), lambda bi, c: (bi, 0, c)),
        out_shape=jax.ShapeDtypeStruct((batch, heads, seq), F32),
        scratch_shapes=[pltpu.VMEM((heads, 1), F32)],
        compiler_params=_params("parallel", "arbitrary"),
        name="fox_gate",
    )(hn, w_in_t, b.reshape(heads, 1))


def _fox_kernel(q_ref, k_ref, v_ref, cq_ref, ck_ref, o_ref, cs_ref, m_ref, l_ref, acc_ref,
                *, tq, nheads, scale):
    i = pl.program_id(2)
    d = FOX_HEAD_DIM
    nchunk = ck_ref.shape[2]

    @pl.when(i == 0)
    def _():
        r = lax.broadcasted_iota(jnp.int32, (LANE, LANE), 0)
        c = lax.broadcasted_iota(jnp.int32, (LANE, LANE), 1)
        for hh in range(nheads):
            def chunk(t, carry, hh=hh):
                row = ck_ref[0, hh, pl.ds(t, 1), :] * LOG2E
                col = jnp.sum(jnp.where(r == c, row, 0.0), axis=1, keepdims=True)
                cs_ref[hh, pl.ds(pl.multiple_of(t * LANE, LANE), LANE), :] = (
                    jnp.broadcast_to(col, (LANE, LANE)))
                return carry
            lax.fori_loop(0, nchunk, chunk, 0, unroll=8)

    m_ref[...] = jnp.full_like(m_ref, -jnp.inf)
    l_ref[...] = jnp.zeros_like(l_ref)
    acc_ref[...] = jnp.zeros_like(acc_ref)

    def step(j, masked):
        start = pl.multiple_of(j * tq, tq)
        heads = range(nheads)
        cols = [slice(hh * d, (hh + 1) * d) for hh in heads]
        scores = [lax.dot_general(k_ref[pl.ds(start, tq), cols[hh]], q_ref[:, cols[hh]], _NT,
                                  preferred_element_type=F32) for hh in heads]
        ys, shifts, alphas = [], [], []
        for hh in heads:
            ct = cq_ref[0, hh:hh + 1, :] * LOG2E
            cs = jnp.tile(cs_ref[hh, pl.ds(start, tq), :], (1, tq // LANE))
            y = scores[hh] * (scale * LOG2E) - cs
            if masked:
                r = lax.broadcasted_iota(jnp.int32, y.shape, 0)
                c = lax.broadcasted_iota(jnp.int32, y.shape, 1)
                y = jnp.where(c >= r, y, NEG)
            m_prev = m_ref[hh]
            m_new = jnp.maximum(m_prev, jnp.max(y, axis=0, keepdims=True) + ct)
            m_ref[hh] = m_new
            ys.append(y)
            shifts.append(m_new - ct)
            alphas.append(jnp.exp2(m_prev - m_new))
        for hh in heads:
            p = jnp.exp2(ys[hh] - shifts[hh])
            l_ref[hh] = alphas[hh] * l_ref[hh] + jnp.sum(p, axis=0, keepdims=True)
            pv = lax.dot_general(v_ref[pl.ds(start, tq), cols[hh]], p.astype(BF16),
                                 (((0,), (0,)), ((), ())),
                                 preferred_element_type=F32)
            acc_ref[hh] = alphas[hh] * acc_ref[hh] + pv

    def body(j, carry):
        step(j, masked=False)
        return carry

    lax.fori_loop(0, i, body, 0)
    step(i, masked=True)
    for hh in range(nheads):
        out = acc_ref[hh] / l_ref[hh]
        o_ref[:, hh * d:(hh + 1) * d] = out.T.astype(o_ref.dtype)


def _fox_attention(qkv, cum, *, batch, seq, heads, tq=FOX_Q_TILE, nheads=FOX_HEADS_PER_STEP):
    d = FOX_HEAD_DIM
    assert seq % tq == 0 and tq % LANE == 0 and heads % nheads == 0, (seq, tq, heads)
    assert qkv.shape == (batch * seq, 3 * heads * d), qkv.shape
    nq = seq // tq
    groups = heads // nheads
    w = nheads * d
    cum_q = cum.reshape(batch * groups, nheads, seq)
    cum_k = cum.reshape(batch * groups, nheads, seq // LANE, LANE)
    return pl.pallas_call(
        functools.partial(_fox_kernel, tq=tq, nheads=nheads, scale=d ** -0.5),
        grid=(batch, groups, nq),
        in_specs=[
            pl.BlockSpec((tq, w), lambda b, g, i: (b * nq + i, g)),
            pl.BlockSpec((seq, w), lambda b, g, i: (b, groups + g)),
            pl.BlockSpec((seq, w), lambda b, g, i: (b, 2 * groups + g)),
            pl.BlockSpec((1, nheads, tq), lambda b, g, i: (b * groups + g, 0, i)),
            pl.BlockSpec((1, nheads, seq // LANE, LANE),
                         lambda b, g, i: (b * groups + g, 0, 0, 0)),
        ],
        out_specs=pl.BlockSpec((tq, w), lambda b, g, i: (b * nq + i, g)),
        out_shape=jax.ShapeDtypeStruct((batch * seq, heads * d), BF16),
        scratch_shapes=[pltpu.VMEM((nheads, seq, LANE), F32),
                        pltpu.VMEM((nheads, 1, tq), F32),
                        pltpu.VMEM((nheads, 1, tq), F32),
                        pltpu.VMEM((nheads, d, tq), F32)],
        compiler_params=_params("parallel", "parallel", "arbitrary"),
        name="fox_attention",
    )(qkv, qkv, qkv, cum_q, cum_k)


def _swa_kernel(slope_ref, sink_ref, dist_ref, q_ref, kp_ref, kc_ref, vp_ref, vc_ref, o_ref,
                *, scale, pairs):
    half = SWA_HEAD_DIM
    npair = SWA_GROUP // 2
    nq = npair * Q_BLOCK
    lo = lax.broadcasted_iota(jnp.int32, (2 * Q_BLOCK, LANE), 1) < half
    dist = dist_ref[...]
    lane_pair = lax.broadcasted_iota(jnp.int32, (1, nq), 1) // Q_BLOCK

    groups = []
    for pp in range(pairs):
        lanes = slice(pp * LANE, (pp + 1) * LANE)
        kband = jnp.concatenate([kp_ref[:, lanes], kc_ref[:, lanes]], axis=0).astype(F32) * scale
        vband = jnp.concatenate([vp_ref[:, lanes], vc_ref[:, lanes]], axis=0).astype(F32)
        kroll = pltpu.roll(kband, half, axis=1)
        vroll = pltpu.roll(vband, half, axis=1)
        for cc in range(2):
            k_src, k_alt = (kband, kroll) if cc == 0 else (kroll, kband)
            v_src, v_alt = (vband, vroll) if cc == 0 else (vroll, vband)
            k_sides = (jnp.where(lo, k_src, 0.0).astype(BF16),
                       jnp.where(lo, 0.0, k_alt).astype(BF16))
            v_sides = (jnp.where(lo, v_src, 0.0).astype(BF16),
                       jnp.where(lo, 0.0, v_alt).astype(BF16))
            kv_head = 2 * pp + cc
            qs = jnp.concatenate(
                [q_ref[:, (kv_head * npair + t) * LANE:(kv_head * npair + t + 1) * LANE]
                 for t in range(npair)], axis=0)
            scores = [lax.dot_general(k_sides[par], qs, _NT, preferred_element_type=F32)
                      for par in range(2)]
            groups.append((kv_head, scores, v_sides))

    first_head = pl.program_id(2) * (2 * pairs * SWA_GROUP)
    for kv_head, scores, v_sides in groups:
        out_t = None
        for par in range(2):
            slope = jnp.zeros((1, nq), F32)
            sink = jnp.zeros((1, nq), F32)
            for t in range(npair):
                head = first_head + kv_head * SWA_GROUP + 2 * t + par
                slope = jnp.where(lane_pair == t, slope_ref[head], slope)
                sink = jnp.where(lane_pair == t, sink_ref[head], sink)
            s = scores[par] - slope * dist
            m = jnp.maximum(jnp.max(s, axis=0, keepdims=True), sink)
            e = jnp.exp(s - m)
            denom = jnp.sum(e, axis=0, keepdims=True) + jnp.exp(sink - m)
            p = (e * (1.0 / denom)).astype(BF16)
            pv = lax.dot_general(v_sides[par], p, (((0,), (0,)), ((), ())),
                                 preferred_element_type=F32)
            out_t = pv if out_t is None else out_t + pv
        out = out_t.T
        for t in range(npair):
            o_ref[:, (kv_head * npair + t) * LANE:(kv_head * npair + t + 1) * LANE] = (
                out[t * Q_BLOCK:(t + 1) * Q_BLOCK].astype(o_ref.dtype))


def _swa_attention(proj, slopes, sinks, *, batch, seq, d_model, max_pairs=SWA_MAX_PAIRS_PER_STEP):
    nb = seq // Q_BLOCK
    kv_heads = d_model // SWA_HEAD_DIM // SWA_GROUP
    pairs = min(max_pairs, kv_heads // 2)
    assert seq % Q_BLOCK == 0 and kv_heads % (2 * pairs) == 0, (seq, kv_heads, pairs)
    assert proj.shape == (batch * seq, d_model + 2 * kv_heads * SWA_HEAD_DIM), proj.shape
    steps = kv_heads // (2 * pairs)
    qw = 2 * pairs * SWA_GROUP * SWA_HEAD_DIM
    kw = pairs * LANE
    k0 = d_model // kw
    v0 = k0 + steps
    prev = lambda b, n, p: b * nb + jnp.maximum(n - 1, 0)
    smem = pl.BlockSpec(memory_space=pltpu.SMEM)
    band = 2 * Q_BLOCK
    nq = (SWA_GROUP // 2) * Q_BLOCK
    return pl.pallas_call(
        functools.partial(_swa_kernel, scale=SWA_HEAD_DIM ** -0.5, pairs=pairs),
        grid=(batch, nb, steps),
        in_specs=[
            smem, smem,
            pl.BlockSpec((None, band, nq), lambda b, n, p: (jnp.minimum(n, 1), 0, 0)),
            pl.BlockSpec((Q_BLOCK, qw), lambda b, n, p: (b * nb + n, p)),
            pl.BlockSpec((Q_BLOCK, kw), lambda b, n, p: (prev(b, n, p), k0 + p)),
            pl.BlockSpec((Q_BLOCK, kw), lambda b, n, p: (b * nb + n, k0 + p)),
            pl.BlockSpec((Q_BLOCK, kw), lambda b, n, p: (prev(b, n, p), v0 + p)),
            pl.BlockSpec((Q_BLOCK, kw), lambda b, n, p: (b * nb + n, v0 + p)),
        ],
        out_specs=pl.BlockSpec((Q_BLOCK, qw), lambda b, n, p: (b * nb + n, p)),
        out_shape=jax.ShapeDtypeStruct((batch * seq, d_model), BF16),
        compiler_params=_params("parallel", "parallel", "arbitrary"),
        name="swa_attention",
    )(slopes, sinks, _swa_distance_table(), proj, proj, proj, proj, proj)


def _swa_distance_table():
    band = 2 * Q_BLOCK
    key = jnp.arange(band)[:, None]
    query = jnp.arange((SWA_GROUP // 2) * Q_BLOCK)[None, :] % Q_BLOCK
    dist = jnp.abs(query - (key - Q_BLOCK)).astype(F32)
    q_chunk = query // CHUNK
    k_chunk = key // CHUNK - WINDOW_CHUNKS
    window = (k_chunk <= q_chunk) & (k_chunk >= q_chunk - WINDOW_CHUNKS)
    first = window & (key >= Q_BLOCK)
    return jnp.stack([jnp.where(first, dist, MASKED_DIST), jnp.where(window, dist, MASKED_DIST)])


def _up_kernel(x_ref, wg_ref, wv_ref, cwg_ref, cwv_ref, cbg_ref, cbv_ref, o_ref,
               ug_ref, uv_ref, res_ref, tailg_ref, tailv_ref, *, tm, seq_tiles):
    i = pl.program_id(0)
    j = pl.program_id(1)
    x = x_ref[...]
    seq_start = (i % seq_tiles) == 0
    nslab = ug_ref.shape[0]
    phases = CONV_ROW_PHASES
    rows = tm // phases

    for w_ref, u_ref, tail_ref in ((wg_ref, ug_ref, tailg_ref), (wv_ref, uv_ref, tailv_ref)):
        u = jnp.dot(x, w_ref[...].astype(BF16), preferred_element_type=F32)
        for s in range(nslab):
            u_ref[s, 0:SUBLANE, :] = jnp.where(seq_start, 0.0, tail_ref[j, s])
            u_ref[s, SUBLANE:SUBLANE + tm, :] = u[:, s * LANE:(s + 1) * LANE]
            tail_ref[j, s] = u_ref[s, tm:tm + SUBLANE, :]

    def conv(u_ref, cw_ref, cb_ref, s, phase):
        lanes = slice(s * LANE, (s + 1) * LANE)
        cw = cw_ref[:, lanes]
        out = cb_ref[:, lanes]
        for tap in range(CONV_WIDTH):
            first = SUBLANE - (CONV_WIDTH - 1) + tap + phase
            out = out + cw[tap:tap + 1, :] * u_ref[s, pl.ds(first, rows, stride=phases), :]
        return out

    for s in range(nslab):
        for phase in range(phases):
            gate = conv(ug_ref, cwg_ref, cbg_ref, s, phase)
            val = conv(uv_ref, cwv_ref, cbv_ref, s, phase)
            res_ref[s, pl.ds(phase, rows, stride=phases), :] = (
                gate * (1.0 / (1.0 + jnp.exp2(gate * -LOG2E))) * val)
        o_ref[:, s * LANE:(s + 1) * LANE] = res_ref[s].astype(o_ref.dtype)


def _conv_ffn_up(hn, w_up, conv_w, conv_b, layer, *, seq, tm=FFN_UP_TILE[0], tn=FFN_UP_TILE[1]):
    m, d = hn.shape
    f = w_up.shape[2] // 2
    assert m % tm == 0 and seq % tm == 0 and f % tn == 0 and tn % LANE == 0, (m, seq, f, tm, tn)
    nj = f // tn
    nslab = tn // LANE
    conv_b = conv_b.reshape(conv_b.shape[0], 1, 2 * f)
    gate = lambda i, j: (layer, 0, j)
    val = lambda i, j: (layer, 0, nj + j)
    slot = pltpu.VMEM((nslab, tm + SUBLANE, LANE), F32)
    tail = pltpu.VMEM((nj, nslab, SUBLANE, LANE), F32)
    return pl.pallas_call(
        functools.partial(_up_kernel, tm=tm, seq_tiles=seq // tm),
        grid=(m // tm, nj),
        in_specs=[
            pl.BlockSpec((tm, d), lambda i, j: (i, 0)),
            pl.BlockSpec((None, d, tn), gate),
            pl.BlockSpec((None, d, tn), val),
            pl.BlockSpec((None, CONV_WIDTH, tn), gate),
            pl.BlockSpec((None, CONV_WIDTH, tn), val),
            pl.BlockSpec((None, 1, tn), gate),
            pl.BlockSpec((None, 1, tn), val),
        ],
        out_specs=pl.BlockSpec((tm, tn), lambda i, j: (i, j)),
        out_shape=jax.ShapeDtypeStruct((m, f), BF16),
        scratch_shapes=[slot, slot, pltpu.VMEM((nslab, tm, LANE), F32), tail, tail],
        compiler_params=_params("arbitrary", "arbitrary"),
        name="conv_ffn_up",
    )(hn, w_up, w_up, conv_w, conv_w, conv_b, conv_b)


def kernel(x, attn_norm_g, fox_w_in, fox_b_f, fox_w_o, swa_w_in, swa_sinks, swa_w_o,
           ffn_norm_g, ffn_w_up, ffn_conv_w, ffn_conv_b, ffn_w_down, final_norm_g):
    batch, seq, d_model = x.shape
    depth = attn_norm_g.shape[0]
    fox_heads = fox_b_f.shape[1]
    swa_heads = swa_sinks.shape[1]
    slopes = jnp.exp2(-8.0 * jnp.arange(1, swa_heads + 1, dtype=F32) / swa_heads)

    fox_w_in_t = jnp.swapaxes(fox_w_in, 1, 2)

    h = x.reshape(batch * seq, d_model)
    for layer in range(depth):
        inst = layer // 2
        hn = _rmsnorm(h, attn_norm_g[layer], BF16)
        if layer % 2 == 0:
            qkv = _mm(hn, fox_w_in_t, inst, 3 * d_model, BF16, tile=QKV_TILE,
                      transposed_w=True, single_buffer_x=True, name="fox_qkv")
            cum = _fox_gate(hn, fox_w_in_t, inst, 3 * d_model, fox_b_f[inst],
                            batch=batch, seq=seq)
            mix = _fox_attention(qkv, cum, batch=batch, seq=seq, heads=fox_heads)
            w_o = fox_w_o
        else:
            proj = _mm(hn, swa_w_in, inst, swa_w_in.shape[2], BF16, tile=QKV_TILE,
                       single_buffer_x=True, name="swa_qkv")
            mix = _swa_attention(proj, slopes, swa_sinks[inst],
                                 batch=batch, seq=seq, d_model=d_model)
            w_o = swa_w_o
        h = _mm(mix, w_o, inst, d_model, F32, tile=ATTN_OUT_TILE, residual=h, name="attn_out")
        hn = _rmsnorm(h, ffn_norm_g[layer], BF16)
        act = _conv_ffn_up(hn, ffn_w_up, ffn_conv_w, ffn_conv_b, layer, seq=seq)
        h = _mm(act, ffn_w_down, layer, d_model, F32, tile=FFN_DOWN_TILE,
                residual=h, single_buffer_x=True, name="ffn_down")
    out = _rmsnorm(h, final_norm_g, F32)
    return out.reshape(batch, seq, d_model)
```

```python
import functools

import jax
import jax.numpy as jnp
from jax import lax
from jax.experimental import pallas as pl
from jax.experimental.pallas import tpu as pltpu

LANE = 128
SUBLANE = 8
VMEM_LIMIT_BYTES = 56 * 1024 * 1024

FOX_HEAD_DIM = 128
SWA_HEAD_DIM = 64
SWA_GROUP = 8
Q_BLOCK = 128
CHUNK = 64
WINDOW_CHUNKS = 2
CONV_WIDTH = 3
EPS = 1e-6
NEG = -1e30
MASKED_DIST = 1e30
LOG2E = 1.4426950408889634

BF16 = jnp.bfloat16
F32 = jnp.float32

QKV_TILE = (2048, 512)
ATTN_OUT_TILE = (1024, 512)
FFN_UP_TILE = (1024, 256)
FFN_DOWN_TILE = (1024, 256)
NORM_ROWS = 512
GATE_SEQ_CHUNK = 1024
FOX_Q_TILE = 512
FOX_HEADS_PER_STEP = 4
SWA_MAX_PAIRS_PER_STEP = 4

_NT = (((1,), (1,)), ((), ()))


def _params(*semantics):
    return pltpu.CompilerParams(
        dimension_semantics=semantics, vmem_limit_bytes=VMEM_LIMIT_BYTES)


def _rmsnorm_kernel(x_ref, g_ref, o_ref):
    x = x_ref[...]
    ms = jnp.mean(x * x, axis=-1, keepdims=True)
    o_ref[...] = (x * lax.rsqrt(ms + EPS) * g_ref[...]).astype(o_ref.dtype)


def _rmsnorm(x, g, out_dtype, *, rows=NORM_ROWS):
    m, d = x.shape
    assert m % rows == 0 and d % LANE == 0, (x.shape, rows)
    return pl.pallas_call(
        _rmsnorm_kernel,
        grid=(m // rows,),
        in_specs=[pl.BlockSpec((rows, d), lambda i: (i, 0)),
                  pl.BlockSpec((1, d), lambda i: (0, 0))],
        out_specs=pl.BlockSpec((rows, d), lambda i: (i, 0)),
        out_shape=jax.ShapeDtypeStruct((m, d), out_dtype),
        compiler_params=_params("parallel"),
        name="rmsnorm",
    )(x, g.reshape(1, d))


def _mm_kernel(*refs, transposed_w, has_residual):
    x_ref, w_ref = refs[:2]
    o_ref = refs[-1]
    w = w_ref[...].astype(BF16)
    if transposed_w:
        acc = lax.dot_general(x_ref[...], w, _NT, preferred_element_type=F32)
    else:
        acc = jnp.dot(x_ref[...], w, preferred_element_type=F32)
    if has_residual:
        acc = refs[2][...] + acc
    o_ref[...] = acc.astype(o_ref.dtype)


def _mm(x, w, layer, n, out_dtype, *, tile, residual=None, transposed_w=False,
        single_buffer_x=False, name="mm"):
    m, k = x.shape
    tm, tn = tile
    k_axis, n_axis = (2, 1) if transposed_w else (1, 2)
    assert w.shape[k_axis] == k and w.shape[n_axis] >= n, (x.shape, w.shape, n)
    assert m % tm == 0 and n % tn == 0, (m, n, tm, tn)
    x_mode = dict(pipeline_mode=pl.Buffered(1)) if single_buffer_x else {}
    if transposed_w:
        w_spec = pl.BlockSpec((None, tn, k), lambda i, j: (layer, j, 0))
    else:
        w_spec = pl.BlockSpec((None, k, tn), lambda i, j: (layer, 0, j))
    in_specs = [pl.BlockSpec((tm, k), lambda i, j: (i, 0), **x_mode), w_spec]
    args = [x, w]
    if residual is not None:
        in_specs.append(pl.BlockSpec((tm, tn), lambda i, j: (i, j)))
        args.append(residual)
    return pl.pallas_call(
        functools.partial(_mm_kernel, transposed_w=transposed_w,
                          has_residual=residual is not None),
        grid=(m // tm, n // tn),
        in_specs=in_specs,
        out_specs=pl.BlockSpec((tm, tn), lambda i, j: (i, j)),
        out_shape=jax.ShapeDtypeStruct((m, n), out_dtype),
        compiler_params=_params("parallel", "arbitrary"),
        name=name,
    )(*args)


def _gate_kernel(x_ref, wt_ref, b_ref, o_ref, carry_ref, *, ts):
    c = pl.program_id(1)

    @pl.when(c == 0)
    def _():
        carry_ref[...] = jnp.zeros_like(carry_ref)

    f = lax.dot_general(wt_ref[...].astype(BF16), x_ref[...], _NT,
                        preferred_element_type=F32) + b_ref[...]
    s = jnp.minimum(f, 0.0) - jnp.log1p(jnp.exp(-jnp.abs(f)))
    lane = lax.broadcasted_iota(jnp.int32, s.shape, 1)
    shift = 1
    while shift < ts:
        s = s + jnp.where(lane >= shift, pltpu.roll(s, shift, axis=1), 0.0)
        shift *= 2
    s = s + carry_ref[...]
    o_ref[0] = s
    carry_ref[...] = s[:, ts - 1:ts]


def _fox_gate(hn, w_in_t, layer, gate_row, b, *, batch, seq, ts=GATE_SEQ_CHUNK):
    heads = b.shape[0]
    d = hn.shape[1]
    assert seq % ts == 0 and heads % SUBLANE == 0, (seq, ts, heads)
    assert gate_row % heads == 0 and gate_row + heads <= w_in_t.shape[1], (gate_row, w_in_t.shape)
    nchunk = seq // ts
    return pl.pallas_call(
        functools.partial(_gate_kernel, ts=ts),
        grid=(batch, nchunk),
        in_specs=[pl.BlockSpec((ts, d), lambda bi, c: (bi * nchunk + c, 0)),
                  pl.BlockSpec((None, heads, d), lambda bi, c: (layer, gate_row // heads, 0)),
                  pl.BlockSpec((heads, 1), lambda bi, c: (0, 0))],
        out_specs=pl.BlockSpec((1, heads, ts), lambda bi, c: (bi, 0, c)),
        out_shape=jax.ShapeDtypeStruct((batch, heads, seq), F32),
        scratch_shapes=[pltpu.VMEM((heads, 1), F32)],
        compiler_params=_params("parallel", "arbitrary"),
        name="fox_gate",
    )(hn, w_in_t, b.reshape(heads, 1))


def _fox_kernel(q_ref, k_ref, v_ref, cq_ref, ck_ref, o_ref, cs_ref, m_ref, l_ref, acc_ref,
                *, tq, nheads, scale):
    i = pl.program_id(2)
    d = FOX_HEAD_DIM
    nchunk = ck_ref.shape[2]

    @pl.when(i == 0)
    def _():
        r = lax.broadcasted_iota(jnp.int32, (LANE, LANE), 0)
        c = lax.broadcasted_iota(jnp.int32, (LANE, LANE), 1)
        for hh in range(nheads):
            def chunk(t, carry, hh=hh):
                row = ck_ref[0, hh, pl.ds(t, 1), :] * LOG2E
                col = jnp.sum(jnp.where(r == c, row, 0.0), axis=1, keepdims=True)
                cs_ref[hh, pl.ds(pl.multiple_of(t * LANE, LANE), LANE), :] = (
                    jnp.broadcast_to(col, (LANE, LANE)))
                return carry
            lax.fori_loop(0, nchunk, chunk, 0, unroll=8)

    m_ref[...] = jnp.full_like(m_ref, -jnp.inf)
    l_ref[...] = jnp.zeros_like(l_ref)
    acc_ref[...] = jnp.zeros_like(acc_ref)

    def step(j, masked):
        start = pl.multiple_of(j * tq, tq)
        heads = range(nheads)
        cols = [slice(hh * d, (hh + 1) * d) for hh in heads]
        scores = [lax.dot_general(k_ref[pl.ds(start, tq), cols[hh]], q_ref[:, cols[hh]], _NT,
                                  preferred_element_type=F32) for hh in heads]
        ys, shifts, alphas = [], [], []
        for hh in heads:
            ct = cq_ref[0, hh:hh + 1, :] * LOG2E
            cs = jnp.tile(cs_ref[hh, pl.ds(start, tq), :], (1, tq // LANE))
            y = scores[hh] * (scale * LOG2E) - cs
            if masked:
                r = lax.broadcasted_iota(jnp.int32, y.shape, 0)
                c = lax.broadcasted_iota(jnp.int32, y.shape, 1)
                y = jnp.where(c >= r, y, NEG)
            m_prev = m_ref[hh]
            m_new = jnp.maximum(m_prev, jnp.max(y, axis=0, keepdims=True) + ct)
            m_ref[hh] = m_new
            ys.append(y)
            shifts.append(m_new - ct)
            alphas.append(jnp.exp2(m_prev - m_new))
        for hh in heads:
            p = jnp.exp2(ys[hh] - shifts[hh])
            l_ref[hh] = alphas[hh] * l_ref[hh] + jnp.sum(p, axis=0, keepdims=True)
            pv = lax.dot_general(v_ref[pl.ds(start, tq), cols[hh]], p.astype(BF16),
                                 (((0,), (0,)), ((), ())),
                                 preferred_element_type=F32)
            acc_ref[hh] = alphas[hh] * acc_ref[hh] + pv

    def body(j, carry):
        step(j, masked=False)
        return carry

    lax.fori_loop(0, i, body, 0)
    step(i, masked=True)
    for hh in range(nheads):
        out = acc_ref[hh] / l_ref[hh]
        o_ref[:, hh * d:(hh + 1) * d] = out.T.astype(o_ref.dtype)


def _fox_attention(qkv, cum, *, batch, seq, heads, tq=FOX_Q_TILE, nheads=FOX_HEADS_PER_STEP):
    d = FOX_HEAD_DIM
    assert seq % tq == 0 and tq % LANE == 0 and heads % nheads == 0, (seq, tq, heads)
    assert qkv.shape == (batch * seq, 3 * heads * d), qkv.shape
    nq = seq // tq
    groups = heads // nheads
    w = nheads * d
    cum_q = cum.reshape(batch * groups, nheads, seq)
    cum_k = cum.reshape(batch * groups, nheads, seq // LANE, LANE)
    return pl.pallas_call(
        functools.partial(_fox_kernel, tq=tq, nheads=nheads, scale=d ** -0.5),
        grid=(batch, groups, nq),
        in_specs=[
            pl.BlockSpec((tq, w), lambda b, g, i: (b * nq + i, g)),
            pl.BlockSpec((seq, w), lambda b, g, i: (b, groups + g)),
            pl.BlockSpec((seq, w), lambda b, g, i: (b, 2 * groups + g)),
            pl.BlockSpec((1, nheads, tq), lambda b, g, i: (b * groups + g, 0, i)),
            pl.BlockSpec((1, nheads, seq // LANE, LANE),
                         lambda b, g, i: (b * groups + g, 0, 0, 0)),
        ],
        out_specs=pl.BlockSpec((tq, w), lambda b, g, i: (b * nq + i, g)),
        out_shape=jax.ShapeDtypeStruct((batch * seq, heads * d), BF16),
        scratch_shapes=[pltpu.VMEM((nheads, seq, LANE), F32),
                        pltpu.VMEM((nheads, 1, tq), F32),
                        pltpu.VMEM((nheads, 1, tq), F32),
                        pltpu.VMEM((nheads, d, tq), F32)],
        compiler_params=_params("parallel", "parallel", "arbitrary"),
        name="fox_attention",
    )(qkv, qkv, qkv, cum_q, cum_k)


def _swa_kernel(slope_ref, sink_ref, dist_ref, q_ref, kp_ref, kc_ref, vp_ref, vc_ref, o_ref,
                *, scale, pairs):
    half = SWA_HEAD_DIM
    npair = SWA_GROUP // 2
    nq = npair * Q_BLOCK
    lo = lax.broadcasted_iota(jnp.int32, (2 * Q_BLOCK, LANE), 1) < half
    dist = dist_ref[...]
    lane_pair = lax.broadcasted_iota(jnp.int32, (1, nq), 1) // Q_BLOCK

    groups = []
    for pp in range(pairs):
        lanes = slice(pp * LANE, (pp + 1) * LANE)
        kband = jnp.concatenate([kp_ref[:, lanes], kc_ref[:, lanes]], axis=0).astype(F32) * scale
        vband = jnp.concatenate([vp_ref[:, lanes], vc_ref[:, lanes]], axis=0).astype(F32)
        kroll = pltpu.roll(kband, half, axis=1)
        vroll = pltpu.roll(vband, half, axis=1)
        for cc in range(2):
            k_src, k_alt = (kband, kroll) if cc == 0 else (kroll, kband)
            v_src, v_alt = (vband, vroll) if cc == 0 else (vroll, vband)
            k_sides = (jnp.where(lo, k_src, 0.0).astype(BF16),
                       jnp.where(lo, 0.0, k_alt).astype(BF16))
            v_sides = (jnp.where(lo, v_src, 0.0).astype(BF16),
                       jnp.where(lo, 0.0, v_alt).astype(BF16))
            kv_head = 2 * pp + cc
            qs = jnp.concatenate(
                [q_ref[:, (kv_head * npair + t) * LANE:(kv_head * npair + t + 1) * LANE]
                 for t in range(npair)], axis=0)
            scores = [lax.dot_general(k_sides[par], qs, _NT, preferred_element_type=F32)
                      for par in range(2)]
            groups.append((kv_head, scores, v_sides))

    first_head = pl.program_id(2) * (2 * pairs * SWA_GROUP)
    for kv_head, scores, v_sides in groups:
        out_t = None
        for par in range(2):
            slope = jnp.zeros((1, nq), F32)
            sink = jnp.zeros((1, nq), F32)
            for t in range(npair):
                head = first_head + kv_head * SWA_GROUP + 2 * t + par
                slope = jnp.where(lane_pair == t, slope_ref[head], slope)
                sink = jnp.where(lane_pair == t, sink_ref[head], sink)
            s = scores[par] - slope * dist
            m = jnp.maximum(jnp.max(s, axis=0, keepdims=True), sink)
            e = jnp.exp(s - m)
            denom = jnp.sum(e, axis=0, keepdims=True) + jnp.exp(sink - m)
            p = (e * (1.0 / denom)).astype(BF16)
            pv = lax.dot_general(v_sides[par], p, (((0,), (0,)), ((), ())),
                                 preferred_element_type=F32)
            out_t = pv if out_t is None else out_t + pv
        out = out_t.T
        for t in range(npair):
            o_ref[:, (kv_head * npair + t) * LANE:(kv_head * npair + t + 1) * LANE] = (
                out[t * Q_BLOCK:(t + 1) * Q_BLOCK].astype(o_ref.dtype))


def _swa_attention(proj, slopes, sinks, *, batch, seq, d_model, max_pairs=SWA_MAX_PAIRS_PER_STEP):
    nb = seq // Q_BLOCK
    kv_heads = d_model // SWA_HEAD_DIM // SWA_GROUP
    pairs = min(max_pairs, kv_heads // 2)
    assert seq % Q_BLOCK == 0 and kv_heads % (2 * pairs) == 0, (seq, kv_heads, pairs)
    assert proj.shape == (batch * seq, d_model + 2 * kv_heads * SWA_HEAD_DIM), proj.shape
    steps = kv_heads // (2 * pairs)
    qw = 2 * pairs * SWA_GROUP * SWA_HEAD_DIM
    kw = pairs * LANE
    k0 = d_model // kw
    v0 = k0 + steps
    prev = lambda b, n, p: b * nb + jnp.maximum(n - 1, 0)
    smem = pl.BlockSpec(memory_space=pltpu.SMEM)
    band = 2 * Q_BLOCK
    nq = (SWA_GROUP // 2) * Q_BLOCK
    return pl.pallas_call(
        functools.partial(_swa_kernel, scale=SWA_HEAD_DIM ** -0.5, pairs=pairs),
        grid=(batch, nb, steps),
        in_specs=[
            smem, smem,
            pl.BlockSpec((None, band, nq), lambda b, n, p: (jnp.minimum(n, 1), 0, 0)),
            pl.BlockSpec((Q_BLOCK, qw), lambda b, n, p: (b * nb + n, p)),
            pl.BlockSpec((Q_BLOCK, kw), lambda b, n, p: (prev(b, n, p), k0 + p)),
            pl.BlockSpec((Q_BLOCK, kw), lambda b, n, p: (b * nb + n, k0 + p)),
            pl.BlockSpec((Q_BLOCK, kw), lambda b, n, p: (prev(b, n, p), v0 + p)),
            pl.BlockSpec((Q_BLOCK, kw), lambda b, n, p: (b * nb + n, v0 + p)),
        ],
        out_specs=pl.BlockSpec((Q_BLOCK, qw), lambda b, n, p: (b * nb + n, p)),
        out_shape=jax.ShapeDtypeStruct((batch * seq, d_model), BF16),
        compiler_params=_params("parallel", "parallel", "arbitrary"),
        name="swa_attention",
    )(slopes, sinks, _swa_distance_table(), proj, proj, proj, proj, proj)


def _swa_distance_table():
    band = 2 * Q_BLOCK
    key = jnp.arange(band)[:, None]
    query = jnp.arange((SWA_GROUP // 2) * Q_BLOCK)[None, :] % Q_BLOCK
    dist = jnp.abs(query - (key - Q_BLOCK)).astype(F32)
    q_chunk = query // CHUNK
    k_chunk = key // CHUNK - WINDOW_CHUNKS
    window = (k_chunk <= q_chunk) & (k_chunk >= q_chunk - WINDOW_CHUNKS)
    first = window & (key >= Q_BLOCK)
    return jnp.stack([jnp.where(first, dist, MASKED_DIST), jnp.where(window, dist, MASKED_DIST)])


def _up_kernel(x_ref, wg_ref, wv_ref, cwg_ref, cwv_ref, cbg_ref, cbv_ref, o_ref,
               ug_ref, uv_ref, res_ref, tailg_ref, tailv_ref, *, tm, seq_tiles):
    i = pl.program_id(0)
    j = pl.program_id(1)
    x = x_ref[...]
    seq_start = (i % seq_tiles) == 0
    nslab = ug_ref.shape[0]
    half = tm // 2

    for w_ref, u_ref, tail_ref in ((wg_ref, ug_ref, tailg_ref), (wv_ref, uv_ref, tailv_ref)):
        u = jnp.dot(x, w_ref[...].astype(BF16), preferred_element_type=F32)
        for s in range(nslab):
            u_ref[s, 0:SUBLANE, :] = jnp.where(seq_start, 0.0, tail_ref[j, s])
            u_ref[s, SUBLANE:SUBLANE + tm, :] = u[:, s * LANE:(s + 1) * LANE]
            tail_ref[j, s] = u_ref[s, tm:tm + SUBLANE, :]

    def conv(u_ref, cw_ref, cb_ref, s):
        lanes = slice(s * LANE, (s + 1) * LANE)
        cw = cw_ref[:, lanes]
        cb = cb_ref[:, lanes]
        first = SUBLANE - (CONV_WIDTH - 1)
        reads = [u_ref[s, pl.ds(first + k, half, stride=2), :] for k in range(CONV_WIDTH + 1)]
        outs = []
        for parity in range(2):
            out = cb
            for tap in range(CONV_WIDTH):
                out = out + cw[tap:tap + 1, :] * reads[tap + parity]
            outs.append(out)
        return outs

    for s in range(nslab):
        gates = conv(ug_ref, cwg_ref, cbg_ref, s)
        vals = conv(uv_ref, cwv_ref, cbv_ref, s)
        for parity in range(2):
            gate, val = gates[parity], vals[parity]
            res_ref[s, pl.ds(parity, half, stride=2), :] = (
                gate * (1.0 / (1.0 + jnp.exp2(gate * -LOG2E))) * val)
        o_ref[:, s * LANE:(s + 1) * LANE] = res_ref[s].astype(o_ref.dtype)


def _conv_ffn_up(hn, w_up, conv_w, conv_b, layer, *, seq, tm=FFN_UP_TILE[0], tn=FFN_UP_TILE[1]):
    m, d = hn.shape
    f = w_up.shape[2] // 2
    assert m % tm == 0 and seq % tm == 0 and f % tn == 0 and tn % LANE == 0, (m, seq, f, tm, tn)
    nj = f // tn
    nslab = tn // LANE
    conv_b = conv_b.reshape(conv_b.shape[0], 1, 2 * f)
    gate = lambda i, j: (layer, 0, j)
    val = lambda i, j: (layer, 0, nj + j)
    slot = pltpu.VMEM((nslab, tm + SUBLANE, LANE), F32)
    tail = pltpu.VMEM((nj, nslab, SUBLANE, LANE), F32)
    return pl.pallas_call(
        functools.partial(_up_kernel, tm=tm, seq_tiles=seq // tm),
        grid=(m // tm, nj),
        in_specs=[
            pl.BlockSpec((tm, d), lambda i, j: (i, 0)),
            pl.BlockSpec((None, d, tn), gate),
            pl.BlockSpec((None, d, tn), val),
            pl.BlockSpec((None, CONV_WIDTH, tn), gate),
            pl.BlockSpec((None, CONV_WIDTH, tn), val),
            pl.BlockSpec((None, 1, tn), gate),
            pl.BlockSpec((None, 1, tn), val),
        ],
        out_specs=pl.BlockSpec((tm, tn), lambda i, j: (i, j)),
        out_shape=jax.ShapeDtypeStruct((m, f), BF16),
        scratch_shapes=[slot, slot, pltpu.VMEM((nslab, tm, LANE), F32), tail, tail],
        compiler_params=_params("arbitrary", "arbitrary"),
        name="conv_ffn_up",
    )(hn, w_up, w_up, conv_w, conv_w, conv_b, conv_b)


def kernel(x, attn_norm_g, fox_w_in, fox_b_f, fox_w_o, swa_w_in, swa_sinks, swa_w_o,
           ffn_norm_g, ffn_w_up, ffn_conv_w, ffn_conv_b, ffn_w_down, final_norm_g):
    batch, seq, d_model = x.shape
    depth = attn_norm_g.shape[0]
    fox_heads = fox_b_f.shape[1]
    swa_heads = swa_sinks.shape[1]
    slopes = jnp.exp2(-8.0 * jnp.arange(1, swa_heads + 1, dtype=F32) / swa_heads)

    fox_w_in_t = jnp.swapaxes(fox_w_in, 1, 2)

    h = x.reshape(batch * seq, d_model)
    for layer in range(depth):
        inst = layer // 2
        hn = _rmsnorm(h, attn_norm_g[layer], BF16)
        if layer % 2 == 0:
            qkv = _mm(hn, fox_w_in_t, inst, 3 * d_model, BF16, tile=QKV_TILE,
                      transposed_w=True, single_buffer_x=True, name="fox_qkv")
            cum = _fox_gate(hn, fox_w_in_t, inst, 3 * d_model, fox_b_f[inst],
                            batch=batch, seq=seq)
            mix = _fox_attention(qkv, cum, batch=batch, seq=seq, heads=fox_heads)
            w_o = fox_w_o
        else:
            proj = _mm(hn, swa_w_in, inst, swa_w_in.shape[2], BF16, tile=QKV_TILE,
                       single_buffer_x=True, name="swa_qkv")
            mix = _swa_attention(proj, slopes, swa_sinks[inst],
                                 batch=batch, seq=seq, d_model=d_model)
            w_o = swa_w_o
        h = _mm(mix, w_o, inst, d_model, F32, tile=ATTN_OUT_TILE, residual=h, name="attn_out")
        hn = _rmsnorm(h, ffn_norm_g[layer], BF16)
        act = _conv_ffn_up(hn, ffn_w_up, ffn_conv_w, ffn_conv_b, layer, seq=seq)
        h = _mm(act, ffn_w_down, layer, d_model, F32, tile=FFN_DOWN_TILE,
                residual=h, single_buffer_x=True, name="ffn_down")
    out = _rmsnorm(h, final_norm_g, F32)
    return out.reshape(batch, seq, d_model)
```

```python
import functools

import jax
import jax.numpy as jnp
from jax import lax
from jax.experimental import pallas as pl
from jax.experimental.pallas import tpu as pltpu

LANE = 128
SUBLANE = 8
VMEM_LIMIT_BYTES = 56 * 1024 * 1024

FOX_HEAD_DIM = 128
SWA_HEAD_DIM = 64
SWA_GROUP = 8
Q_BLOCK = 128
CHUNK = 64
WINDOW_CHUNKS = 2
CONV_WIDTH = 3
EPS = 1e-6
NEG = -1e30
MASKED_DIST = 1e30
LOG2E = 1.4426950408889634
FOX_Q_SCALE = FOX_HEAD_DIM ** -0.5 * LOG2E

BF16 = jnp.bfloat16
F32 = jnp.float32

QKV_TILE = (2048, 512)
ATTN_OUT_TILE = (1024, 512)
FFN_UP_TILE = (1024, 256)
FFN_DOWN_TILE = (1024, 256)
NORM_ROWS = 512
GATE_SEQ_CHUNK = 1024
FOX_Q_TILE = 512
FOX_HEADS_PER_STEP = 4
SWA_MAX_PAIRS_PER_STEP = 4

_NT = (((1,), (1,)), ((), ()))


def _params(*semantics):
    return pltpu.CompilerParams(
        dimension_semantics=semantics, vmem_limit_bytes=VMEM_LIMIT_BYTES)


def _rmsnorm_kernel(x_ref, g_ref, o_ref):
    x = x_ref[...]
    ms = jnp.mean(x * x, axis=-1, keepdims=True)
    o_ref[...] = (x * lax.rsqrt(ms + EPS) * g_ref[...]).astype(o_ref.dtype)


def _rmsnorm(x, g, out_dtype, *, rows=NORM_ROWS):
    m, d = x.shape
    assert m % rows == 0 and d % LANE == 0, (x.shape, rows)
    return pl.pallas_call(
        _rmsnorm_kernel,
        grid=(m // rows,),
        in_specs=[pl.BlockSpec((rows, d), lambda i: (i, 0)),
                  pl.BlockSpec((1, d), lambda i: (0, 0))],
        out_specs=pl.BlockSpec((rows, d), lambda i: (i, 0)),
        out_shape=jax.ShapeDtypeStruct((m, d), out_dtype),
        compiler_params=_params("parallel"),
        name="rmsnorm",
    )(x, g.reshape(1, d))


def _mm_kernel(*refs, transposed_w, has_residual, scaled_blocks, scale):
    x_ref, w_ref = refs[:2]
    o_ref = refs[-1]
    w = w_ref[...].astype(BF16)
    if transposed_w:
        acc = lax.dot_general(x_ref[...], w, _NT, preferred_element_type=F32)
    else:
        acc = jnp.dot(x_ref[...], w, preferred_element_type=F32)
    if scaled_blocks:
        acc = acc * jnp.where(pl.program_id(1) < scaled_blocks, scale, 1.0)
    if has_residual:
        acc = refs[2][...] + acc
    o_ref[...] = acc.astype(o_ref.dtype)


def _mm(x, w, layer, n, out_dtype, *, tile, residual=None, transposed_w=False,
        single_buffer_x=False, scaled_cols=0, scale=1.0, name="mm"):
    m, k = x.shape
    tm, tn = tile
    assert scaled_cols % tn == 0, (scaled_cols, tn)
    k_axis, n_axis = (2, 1) if transposed_w else (1, 2)
    assert w.shape[k_axis] == k and w.shape[n_axis] >= n, (x.shape, w.shape, n)
    assert m % tm == 0 and n % tn == 0, (m, n, tm, tn)
    x_mode = dict(pipeline_mode=pl.Buffered(1)) if single_buffer_x else {}
    if transposed_w:
        w_spec = pl.BlockSpec((None, tn, k), lambda i, j: (layer, j, 0))
    else:
        w_spec = pl.BlockSpec((None, k, tn), lambda i, j: (layer, 0, j))
    in_specs = [pl.BlockSpec((tm, k), lambda i, j: (i, 0), **x_mode), w_spec]
    args = [x, w]
    if residual is not None:
        in_specs.append(pl.BlockSpec((tm, tn), lambda i, j: (i, j)))
        args.append(residual)
    return pl.pallas_call(
        functools.partial(_mm_kernel, transposed_w=transposed_w,
                          has_residual=residual is not None,
                          scaled_blocks=scaled_cols // tn, scale=scale),
        grid=(m // tm, n // tn),
        in_specs=in_specs,
        out_specs=pl.BlockSpec((tm, tn), lambda i, j: (i, j)),
        out_shape=jax.ShapeDtypeStruct((m, n), out_dtype),
        compiler_params=_params("parallel", "arbitrary"),
        name=name,
    )(*args)


def _gate_kernel(x_ref, wt_ref, b_ref, o_ref, carry_ref, *, ts):
    c = pl.program_id(1)

    @pl.when(c == 0)
    def _():
        carry_ref[...] = jnp.zeros_like(carry_ref)

    f = lax.dot_general(wt_ref[...].astype(BF16), x_ref[...], _NT,
                        preferred_element_type=F32) + b_ref[...]
    s = jnp.minimum(f, 0.0) - jnp.log1p(jnp.exp(-jnp.abs(f)))
    lane = lax.broadcasted_iota(jnp.int32, s.shape, 1)
    shift = 1
    while shift < ts:
        s = s + jnp.where(lane >= shift, pltpu.roll(s, shift, axis=1), 0.0)
        shift *= 2
    s = s + carry_ref[...]
    o_ref[0] = s
    carry_ref[...] = s[:, ts - 1:ts]


def _fox_gate(hn, w_in_t, layer, gate_row, b, *, batch, seq, ts=GATE_SEQ_CHUNK):
    heads = b.shape[0]
    d = hn.shape[1]
    assert seq % ts == 0 and heads % SUBLANE == 0, (seq, ts, heads)
    assert gate_row % heads == 0 and gate_row + heads <= w_in_t.shape[1], (gate_row, w_in_t.shape)
    nchunk = seq // ts
    return pl.pallas_call(
        functools.partial(_gate_kernel, ts=ts),
        grid=(batch, nchunk),
        in_specs=[pl.BlockSpec((ts, d), lambda bi, c: (bi * nchunk + c, 0)),
                  pl.BlockSpec((None, heads, d), lambda bi, c: (layer, gate_row // heads, 0)),
                  pl.BlockSpec((heads, 1), lambda bi, c: (0, 0))],
        out_specs=pl.BlockSpec((1, heads, ts), lambda bi, c: (bi, 0, c)),
        out_shape=jax.ShapeDtypeStruct((batch, heads, seq), F32),
        scratch_shapes=[pltpu.VMEM((heads, 1), F32)],
        compiler_params=_params("parallel", "arbitrary"),
        name="fox_gate",
    )(hn, w_in_t, b.reshape(heads, 1))


def _fox_kernel(q_ref, k_ref, v_ref, cq_ref, ck_ref, o_ref, cs_ref, m_ref, l_ref, acc_ref,
                *, tq, nheads):
    i = pl.program_id(2)
    d = FOX_HEAD_DIM
    nchunk = ck_ref.shape[2]

    @pl.when(i == 0)
    def _():
        r = lax.broadcasted_iota(jnp.int32, (LANE, LANE), 0)
        c = lax.broadcasted_iota(jnp.int32, (LANE, LANE), 1)
        for hh in range(nheads):
            def chunk(t, carry, hh=hh):
                row = ck_ref[0, hh, pl.ds(t, 1), :] * LOG2E
                col = jnp.sum(jnp.where(r == c, row, 0.0), axis=1, keepdims=True)
                cs_ref[hh, pl.ds(pl.multiple_of(t * LANE, LANE), LANE), :] = (
                    jnp.broadcast_to(col, (LANE, LANE)))
                return carry
            lax.fori_loop(0, nchunk, chunk, 0, unroll=8)

    m_ref[...] = jnp.full_like(m_ref, -jnp.inf)
    l_ref[...] = jnp.zeros_like(l_ref)
    acc_ref[...] = jnp.zeros_like(acc_ref)

    def step(j, masked):
        start = pl.multiple_of(j * tq, tq)
        heads = range(nheads)
        cols = [slice(hh * d, (hh + 1) * d) for hh in heads]
        scores = [lax.dot_general(k_ref[pl.ds(start, tq), cols[hh]], q_ref[:, cols[hh]], _NT,
                                  preferred_element_type=F32) for hh in heads]
        ys, shifts, alphas = [], [], []
        for hh in heads:
            ct = cq_ref[0, hh:hh + 1, :] * LOG2E
            cs = jnp.tile(cs_ref[hh, pl.ds(start, tq), :], (1, tq // LANE))
            y = scores[hh] - cs
            if masked:
                r = lax.broadcasted_iota(jnp.int32, y.shape, 0)
                c = lax.broadcasted_iota(jnp.int32, y.shape, 1)
                y = jnp.where(c >= r, y, NEG)
            m_prev = m_ref[hh]
            m_new = jnp.maximum(m_prev, jnp.max(y, axis=0, keepdims=True) + ct)
            m_ref[hh] = m_new
            ys.append(y)
            shifts.append(m_new - ct)
            alphas.append(jnp.exp2(m_prev - m_new))
        for hh in heads:
            p = jnp.exp2(ys[hh] - shifts[hh])
            l_ref[hh] = alphas[hh] * l_ref[hh] + jnp.sum(p, axis=0, keepdims=True)
            pv = lax.dot_general(v_ref[pl.ds(start, tq), cols[hh]], p.astype(BF16),
                                 (((0,), (0,)), ((), ())),
                                 preferred_element_type=F32)
            acc_ref[hh] = alphas[hh] * acc_ref[hh] + pv

    def body(j, carry):
        step(j, masked=False)
        return carry

    lax.fori_loop(0, i, body, 0)
    step(i, masked=True)
    for hh in range(nheads):
        out = acc_ref[hh] / l_ref[hh]
        o_ref[:, hh * d:(hh + 1) * d] = out.T.astype(o_ref.dtype)


def _fox_attention(qkv, cum, *, batch, seq, heads, tq=FOX_Q_TILE, nheads=FOX_HEADS_PER_STEP):
    d = FOX_HEAD_DIM
    assert seq % tq == 0 and tq % LANE == 0 and heads % nheads == 0, (seq, tq, heads)
    assert qkv.shape == (batch * seq, 3 * heads * d), qkv.shape
    nq = seq // tq
    groups = heads // nheads
    w = nheads * d
    cum_q = cum.reshape(batch * groups, nheads, seq)
    cum_k = cum.reshape(batch * groups, nheads, seq // LANE, LANE)
    return pl.pallas_call(
        functools.partial(_fox_kernel, tq=tq, nheads=nheads),
        grid=(batch, groups, nq),
        in_specs=[
            pl.BlockSpec((tq, w), lambda b, g, i: (b * nq + i, g)),
            pl.BlockSpec((seq, w), lambda b, g, i: (b, groups + g)),
            pl.BlockSpec((seq, w), lambda b, g, i: (b, 2 * groups + g)),
            pl.BlockSpec((1, nheads, tq), lambda b, g, i: (b * groups + g, 0, i)),
            pl.BlockSpec((1, nheads, seq // LANE, LANE),
                         lambda b, g, i: (b * groups + g, 0, 0, 0)),
        ],
        out_specs=pl.BlockSpec((tq, w), lambda b, g, i: (b * nq + i, g)),
        out_shape=jax.ShapeDtypeStruct((batch * seq, heads * d), BF16),
        scratch_shapes=[pltpu.VMEM((nheads, seq, LANE), F32),
                        pltpu.VMEM((nheads, 1, tq), F32),
                        pltpu.VMEM((nheads, 1, tq), F32),
                        pltpu.VMEM((nheads, d, tq), F32)],
        compiler_params=_params("parallel", "parallel", "arbitrary"),
        name="fox_attention",
    )(qkv, qkv, qkv, cum_q, cum_k)


def _swa_kernel(slope_ref, sink_ref, dist_ref, q_ref, kp_ref, kc_ref, vp_ref, vc_ref, o_ref,
                *, scale, pairs):
    half = SWA_HEAD_DIM
    npair = SWA_GROUP // 2
    nq = npair * Q_BLOCK
    lo = lax.broadcasted_iota(jnp.int32, (2 * Q_BLOCK, LANE), 1) < half
    dist = dist_ref[...]
    lane_pair = lax.broadcasted_iota(jnp.int32, (1, nq), 1) // Q_BLOCK

    groups = []
    for pp in range(pairs):
        lanes = slice(pp * LANE, (pp + 1) * LANE)
        kband = jnp.concatenate([kp_ref[:, lanes], kc_ref[:, lanes]], axis=0).astype(F32) * scale
        vband = jnp.concatenate([vp_ref[:, lanes], vc_ref[:, lanes]], axis=0).astype(F32)
        kroll = pltpu.roll(kband, half, axis=1)
        vroll = pltpu.roll(vband, half, axis=1)
        for cc in range(2):
            k_src, k_alt = (kband, kroll) if cc == 0 else (kroll, kband)
            v_src, v_alt = (vband, vroll) if cc == 0 else (vroll, vband)
            k_sides = (jnp.where(lo, k_src, 0.0).astype(BF16),
                       jnp.where(lo, 0.0, k_alt).astype(BF16))
            v_sides = (jnp.where(lo, v_src, 0.0).astype(BF16),
                       jnp.where(lo, 0.0, v_alt).astype(BF16))
            kv_head = 2 * pp + cc
            qs = jnp.concatenate(
                [q_ref[:, (kv_head * npair + t) * LANE:(kv_head * npair + t + 1) * LANE]
                 for t in range(npair)], axis=0)
            scores = [lax.dot_general(k_sides[par], qs, _NT, preferred_element_type=F32)
                      for par in range(2)]
            groups.append((kv_head, scores, v_sides))

    first_head = pl.program_id(2) * (2 * pairs * SWA_GROUP)
    for kv_head, scores, v_sides in groups:
        out_t = None
        for par in range(2):
            slope = jnp.zeros((1, nq), F32)
            sink = jnp.zeros((1, nq), F32)
            for t in range(npair):
                head = first_head + kv_head * SWA_GROUP + 2 * t + par
                slope = jnp.where(lane_pair == t, slope_ref[head], slope)
                sink = jnp.where(lane_pair == t, sink_ref[head], sink)
            s = scores[par] - slope * dist
            m = jnp.maximum(jnp.max(s, axis=0, keepdims=True), sink)
            e = jnp.exp(s - m)
            denom = jnp.sum(e, axis=0, keepdims=True) + jnp.exp(sink - m)
            p = (e * (1.0 / denom)).astype(BF16)
            pv = lax.dot_general(v_sides[par], p, (((0,), (0,)), ((), ())),
                                 preferred_element_type=F32)
            out_t = pv if out_t is None else out_t + pv
        out = out_t.T
        for t in range(npair):
            o_ref[:, (kv_head * npair + t) * LANE:(kv_head * npair + t + 1) * LANE] = (
                out[t * Q_BLOCK:(t + 1) * Q_BLOCK].astype(o_ref.dtype))


def _swa_attention(proj, slopes, sinks, *, batch, seq, d_model, max_pairs=SWA_MAX_PAIRS_PER_STEP):
    nb = seq // Q_BLOCK
    kv_heads = d_model // SWA_HEAD_DIM // SWA_GROUP
    pairs = min(max_pairs, kv_heads // 2)
    assert seq % Q_BLOCK == 0 and kv_heads % (2 * pairs) == 0, (seq, kv_heads, pairs)
    assert proj.shape == (batch * seq, d_model + 2 * kv_heads * SWA_HEAD_DIM), proj.shape
    steps = kv_heads // (2 * pairs)
    qw = 2 * pairs * SWA_GROUP * SWA_HEAD_DIM
    kw = pairs * LANE
    k0 = d_model // kw
    v0 = k0 + steps
    prev = lambda b, n, p: b * nb + jnp.maximum(n - 1, 0)
    smem = pl.BlockSpec(memory_space=pltpu.SMEM)
    band = 2 * Q_BLOCK
    nq = (SWA_GROUP // 2) * Q_BLOCK
    return pl.pallas_call(
        functools.partial(_swa_kernel, scale=SWA_HEAD_DIM ** -0.5, pairs=pairs),
        grid=(batch, nb, steps),
        in_specs=[
            smem, smem,
            pl.BlockSpec((None, band, nq), lambda b, n, p: (jnp.minimum(n, 1), 0, 0)),
            pl.BlockSpec((Q_BLOCK, qw), lambda b, n, p: (b * nb + n, p)),
            pl.BlockSpec((Q_BLOCK, kw), lambda b, n, p: (prev(b, n, p), k0 + p)),
            pl.BlockSpec((Q_BLOCK, kw), lambda b, n, p: (b * nb + n, k0 + p)),
            pl.BlockSpec((Q_BLOCK, kw), lambda b, n, p: (prev(b, n, p), v0 + p)),
            pl.BlockSpec((Q_BLOCK, kw), lambda b, n, p: (b * nb + n, v0 + p)),
        ],
        out_specs=pl.BlockSpec((Q_BLOCK, qw), lambda b, n, p: (b * nb + n, p)),
        out_shape=jax.ShapeDtypeStruct((batch * seq, d_model), BF16),
        compiler_params=_params("parallel", "parallel", "arbitrary"),
        name="swa_attention",
    )(slopes, sinks, _swa_distance_table(), proj, proj, proj, proj, proj)


def _swa_distance_table():
    band = 2 * Q_BLOCK
    key = jnp.arange(band)[:, None]
    query = jnp.arange((SWA_GROUP // 2) * Q_BLOCK)[None, :] % Q_BLOCK
    dist = jnp.abs(query - (key - Q_BLOCK)).astype(F32)
    q_chunk = query // CHUNK
    k_chunk = key // CHUNK - WINDOW_CHUNKS
    window = (k_chunk <= q_chunk) & (k_chunk >= q_chunk - WINDOW_CHUNKS)
    first = window & (key >= Q_BLOCK)
    return jnp.stack([jnp.where(first, dist, MASKED_DIST), jnp.where(window, dist, MASKED_DIST)])


def _up_kernel(x_ref, wg_ref, wv_ref, cwg_ref, cwv_ref, cbg_ref, cbv_ref, o_ref,
               ug_ref, uv_ref, res_ref, tailg_ref, tailv_ref, *, tm, seq_tiles):
    i = pl.program_id(0)
    j = pl.program_id(1)
    x = x_ref[...]
    seq_start = (i % seq_tiles) == 0
    nslab = ug_ref.shape[0]
    half = tm // 2

    for w_ref, u_ref, tail_ref in ((wg_ref, ug_ref, tailg_ref), (wv_ref, uv_ref, tailv_ref)):
        u = jnp.dot(x, w_ref[...].astype(BF16), preferred_element_type=F32)
        for s in range(nslab):
            u_ref[s, 0:SUBLANE, :] = jnp.where(seq_start, 0.0, tail_ref[j, s])
            u_ref[s, SUBLANE:SUBLANE + tm, :] = u[:, s * LANE:(s + 1) * LANE]
            tail_ref[j, s] = u_ref[s, tm:tm + SUBLANE, :]

    def conv(u_ref, cw_ref, cb_ref, s):
        lanes = slice(s * LANE, (s + 1) * LANE)
        cw = cw_ref[:, lanes]
        cb = cb_ref[:, lanes]
        first = SUBLANE - (CONV_WIDTH - 1)
        reads = [u_ref[s, pl.ds(first + k, half, stride=2), :] for k in range(CONV_WIDTH + 1)]
        outs = []
        for parity in range(2):
            out = cb
            for tap in range(CONV_WIDTH):
                out = out + cw[tap:tap + 1, :] * reads[tap + parity]
            outs.append(out)
        return outs

    for s in range(nslab):
        gates = conv(ug_ref, cwg_ref, cbg_ref, s)
        vals = conv(uv_ref, cwv_ref, cbv_ref, s)
        for parity in range(2):
            gate, val = gates[parity], vals[parity]
            res_ref[s, pl.ds(parity, half, stride=2), :] = (
                gate * (1.0 / (1.0 + jnp.exp2(gate * -LOG2E))) * val)
        o_ref[:, s * LANE:(s + 1) * LANE] = res_ref[s].astype(o_ref.dtype)


def _conv_ffn_up(hn, w_up, conv_w, conv_b, layer, *, seq, tm=FFN_UP_TILE[0], tn=FFN_UP_TILE[1]):
    m, d = hn.shape
    f = w_up.shape[2] // 2
    assert m % tm == 0 and seq % tm == 0 and f % tn == 0 and tn % LANE == 0, (m, seq, f, tm, tn)
    nj = f // tn
    nslab = tn // LANE
    conv_b = conv_b.reshape(conv_b.shape[0], 1, 2 * f)
    gate = lambda i, j: (layer, 0, j)
    val = lambda i, j: (layer, 0, nj + j)
    slot = pltpu.VMEM((nslab, tm + SUBLANE, LANE), F32)
    tail = pltpu.VMEM((nj, nslab, SUBLANE, LANE), F32)
    return pl.pallas_call(
        functools.partial(_up_kernel, tm=tm, seq_tiles=seq // tm),
        grid=(m // tm, nj),
        in_specs=[
            pl.BlockSpec((tm, d), lambda i, j: (i, 0)),
            pl.BlockSpec((None, d, tn), gate),
            pl.BlockSpec((None, d, tn), val),
            pl.BlockSpec((None, CONV_WIDTH, tn), gate),
            pl.BlockSpec((None, CONV_WIDTH, tn), val),
            pl.BlockSpec((None, 1, tn), gate),
            pl.BlockSpec((None, 1, tn), val),
        ],
        out_specs=pl.BlockSpec((tm, tn), lambda i, j: (i, j)),
        out_shape=jax.ShapeDtypeStruct((m, f), BF16),
        scratch_shapes=[slot, slot, pltpu.VMEM((nslab, tm, LANE), F32), tail, tail],
        compiler_params=_params("arbitrary", "arbitrary"),
        name="conv_ffn_up",
    )(hn, w_up, w_up, conv_w, conv_w, conv_b, conv_b)


def kernel(x, attn_norm_g, fox_w_in, fox_b_f, fox_w_o, swa_w_in, swa_sinks, swa_w_o,
           ffn_norm_g, ffn_w_up, ffn_conv_w, ffn_conv_b, ffn_w_down, final_norm_g):
    batch, seq, d_model = x.shape
    depth = attn_norm_g.shape[0]
    fox_heads = fox_b_f.shape[1]
    swa_heads = swa_sinks.shape[1]
    slopes = jnp.exp2(-8.0 * jnp.arange(1, swa_heads + 1, dtype=F32) / swa_heads)

    fox_w_in_t = jnp.swapaxes(fox_w_in, 1, 2)

    h = x.reshape(batch * seq, d_model)
    for layer in range(depth):
        inst = layer // 2
        hn = _rmsnorm(h, attn_norm_g[layer], BF16)
        if layer % 2 == 0:
            qkv = _mm(hn, fox_w_in_t, inst, 3 * d_model, BF16, tile=QKV_TILE,
                      transposed_w=True, single_buffer_x=True,
                      scaled_cols=d_model, scale=FOX_Q_SCALE, name="fox_qkv")
            cum = _fox_gate(hn, fox_w_in_t, inst, 3 * d_model, fox_b_f[inst],
                            batch=batch, seq=seq)
            mix = _fox_attention(qkv, cum, batch=batch, seq=seq, heads=fox_heads)
            w_o = fox_w_o
        else:
            proj = _mm(hn, swa_w_in, inst, swa_w_in.shape[2], BF16, tile=QKV_TILE,
                       single_buffer_x=True, name="swa_qkv")
            mix = _swa_attention(proj, slopes, swa_sinks[inst],
                                 batch=batch, seq=seq, d_model=d_model)
            w_o = swa_w_o
        h = _mm(mix, w_o, inst, d_model, F32, tile=ATTN_OUT_TILE, residual=h, name="attn_out")
        hn = _rmsnorm(h, ffn_norm_g[layer], BF16)
        act = _conv_ffn_up(hn, ffn_w_up, ffn_conv_w, ffn_conv_b, layer, seq=seq)
        h = _mm(act, ffn_w_down, layer, d_model, F32, tile=FFN_DOWN_TILE,
                residual=h, single_buffer_x=True, name="ffn_down")
    out = _rmsnorm(h, final_norm_g, F32)
    return out.reshape(batch, seq, d_model)
```
